```python
import jax, jax.numpy as jnp
from jax import lax
import numpy as np

D_MODEL = 2048
BATCH = 1
SEQ = 8192
DEPTH = 1

CHUNK = 64
EPS = 1e-6
CONV_CH = 1024
CONV_GROUPS = 8
CONV_WIDTH = 31
MLA_HEADS = 8
QK_NOPE = 128
QK_ROPE = 64
QK_HEAD = QK_NOPE + QK_ROPE
V_HEAD = 128
Q_LORA = 768
KV_LORA = 512
ROPE_THETA = 10000.0
Q_BLOCK = 128
MIX_WIDTH = CONV_CH + MLA_HEADS * V_HEAD
IN_COLS = 2 * CONV_CH + Q_LORA + KV_LORA + QK_ROPE
PEER_HEADS = 8
N_KEYS = 128
N_EXPERTS = N_KEYS * N_KEYS
PEER_TOPK = 16
PEER_QDIM = 256
PEER_HALF = PEER_QDIM // 2
TOKEN_BLOCK = 128

kernel_name = "hybrid_conv_mla_peer_adaln_block"


def rmsnorm(x, g):
    xf = x.astype(jnp.float32)
    y = xf * lax.rsqrt(jnp.mean(xf * xf, axis=-1, keepdims=True) + EPS)
    return (y * g.astype(jnp.float32)).astype(x.dtype)


def layernorm(x, g, b):
    xf = x.astype(jnp.float32)
    mu = jnp.mean(xf, axis=-1, keepdims=True)
    var = jnp.mean(jnp.square(xf - mu), axis=-1, keepdims=True)
    y = (xf - mu) * lax.rsqrt(var + EPS)
    return (y * g.astype(jnp.float32) + b.astype(jnp.float32)).astype(x.dtype)


def apply_rope(x, cos, sin):
    x1, x2 = jnp.split(x.astype(jnp.float32), 2, axis=-1)
    out = jnp.concatenate([x1 * cos - x2 * sin, x2 * cos + x1 * sin], axis=-1)
    return out.astype(x.dtype)


def conv_module(a, dw_w, dw_b, ln_g, ln_b):
    val, gate = jnp.split(a, 2, axis=-1)
    h = val * jax.nn.sigmoid(gate)
    h = lax.conv_general_dilated(
        h, dw_w[:, None, :], window_strides=(1,), padding=[(CONV_WIDTH - 1, 0)],
        dimension_numbers=("NWC", "WIO", "NWC"), feature_group_count=CONV_CH) + dw_b
    h = layernorm(h, ln_g, ln_b)
    return jax.nn.silu(h)


def chunk_causal_attention(q, k, v):
    B, H, S, Dk = q.shape
    nb = S // Q_BLOCK
    qb = q.reshape(B, H, nb, Q_BLOCK, Dk).transpose(2, 0, 1, 3, 4)
    k_chunk = jnp.arange(S) // CHUNK
    scale = QK_HEAD ** -0.5

    def one_block(args):
        qi, bi = args
        s = jnp.einsum("bhqd,bhkd->bhqk", qi, k).astype(jnp.float32) * scale
        q_chunk = (bi * Q_BLOCK + jnp.arange(Q_BLOCK)) // CHUNK
        mask = k_chunk[None, :] <= q_chunk[:, None]
        s = jnp.where(mask, s, -jnp.inf)
        p = jax.nn.softmax(s, axis=-1).astype(v.dtype)
        return jnp.einsum("bhqk,bhkd->bhqd", p, v)

    out = lax.map(one_block, (qb, jnp.arange(nb)))
    return out.transpose(1, 2, 0, 3, 4).reshape(B, H, S, v.shape[-1])


def mla(q_lat, kv_lat, k_rope, positions, q_a_norm_g, w_uq, kv_a_norm_g, w_ukv, q_norm_g, k_norm_g):
    B, S, _ = q_lat.shape
    q = (rmsnorm(q_lat, q_a_norm_g) @ w_uq).reshape(B, S, MLA_HEADS, QK_HEAD)
    kv = (rmsnorm(kv_lat, kv_a_norm_g) @ w_ukv).reshape(B, S, MLA_HEADS, QK_NOPE + V_HEAD)
    k_nope, v = kv[..., :QK_NOPE], kv[..., QK_NOPE:]
    k_r = jnp.broadcast_to(k_rope[:, :, None, :], (B, S, MLA_HEADS, QK_ROPE))
    k = jnp.concatenate([k_nope, k_r], axis=-1)
    q = rmsnorm(q, q_norm_g).transpose(0, 2, 1, 3)
    k = rmsnorm(k, k_norm_g).transpose(0, 2, 1, 3)
    v = v.transpose(0, 2, 1, 3)
    inv_freq = ROPE_THETA ** (-jnp.arange(0, QK_ROPE, 2, dtype=jnp.float32) / QK_ROPE)
    ang = positions.astype(jnp.float32)[..., None] * inv_freq
    cos, sin = jnp.cos(ang)[:, None], jnp.sin(ang)[:, None]
    q = jnp.concatenate([q[..., :QK_NOPE], apply_rope(q[..., QK_NOPE:], cos, sin)], axis=-1)
    k = jnp.concatenate([k[..., :QK_NOPE], apply_rope(k[..., QK_NOPE:], cos, sin)], axis=-1)
    o = chunk_causal_attention(q, k, v)
    return o.transpose(0, 2, 1, 3).reshape(B, S, MLA_HEADS * V_HEAD)


def peer(h, wq, k1, k2, u, v):
    B, S, D = h.shape
    T = B * S
    xt = h.reshape(T, D)
    q = (xt @ wq).reshape(T, PEER_HEADS, 2, PEER_HALF)
    s1 = jnp.einsum("thd,hnd->thn", q[:, :, 0], k1).astype(jnp.float32)
    s2 = jnp.einsum("thd,hnd->thn", q[:, :, 1], k2).astype(jnp.float32)
    v1, i1 = lax.top_k(s1, PEER_TOPK)
    v2, i2 = lax.top_k(s2, PEER_TOPK)
    cand = (v1[..., :, None] + v2[..., None, :]).reshape(T, PEER_HEADS, PEER_TOPK * PEER_TOPK)
    best, flat = lax.top_k(cand, PEER_TOPK)
    e1 = jnp.take_along_axis(i1, flat // PEER_TOPK, axis=-1)
    e2 = jnp.take_along_axis(i2, flat % PEER_TOPK, axis=-1)
    experts = e1 * N_KEYS + e2
    gates = jax.nn.softmax(best, axis=-1).astype(h.dtype)
    nb = T // TOKEN_BLOCK

    def one_block(args):
        xb, eb, gb = args
        a = jnp.einsum("td,thkd->thk", xb, u[eb])
        act = gb * jax.nn.gelu(a)
        return jnp.einsum("thk,thkd->td", act, v[eb])

    y = lax.map(one_block, (xt.reshape(nb, TOKEN_BLOCK, D),
                            experts.reshape(nb, TOKEN_BLOCK, PEER_HEADS, PEER_TOPK),
                            gates.reshape(nb, TOKEN_BLOCK, PEER_HEADS, PEER_TOPK)))
    return y.reshape(B, S, D)


def setup_inputs(seed: int = 0) -> dict:
    key = jax.random.key(seed)
    ks = jax.random.split(key, 24)
    L, D = DEPTH, D_MODEL

    def nrm(k, shape, scale):
        return jax.random.normal(k, shape, dtype=jnp.float32) * scale

    positions = (jax.random.randint(ks[2], (BATCH, 1), 0, 16) * CHUNK
                 + jnp.arange(SEQ)[None, :]).astype(jnp.int32)
    return {
        "x": nrm(ks[0], (BATCH, SEQ, D), 1.0),
        "c": nrm(ks[1], (BATCH, D), 1.0),
        "positions": positions,
        "w_ada": nrm(ks[3], (L, D, 6 * D), 0.5 * D ** -0.5),
        "b_ada": nrm(ks[4], (L, 6 * D), 0.02),
        "norm1_g": 1.0 + nrm(ks[5], (L, D), 0.02),
        "w_in": nrm(ks[6], (L, D, IN_COLS), D ** -0.5),
        "conv_dw_w": nrm(ks[7], (L, CONV_WIDTH, CONV_CH), CONV_WIDTH ** -0.5),
        "conv_dw_b": nrm(ks[8], (L, CONV_CH), 0.02),
        "conv_ln_g": 1.0 + nrm(ks[9], (L, CONV_CH), 0.02),
        "conv_ln_b": nrm(ks[10], (L, CONV_CH), 0.02),
        "q_a_norm_g": 1.0 + nrm(ks[11], (L, Q_LORA), 0.02),
        "w_uq": nrm(ks[12], (L, Q_LORA, MLA_HEADS * QK_HEAD), Q_LORA ** -0.5),
        "kv_a_norm_g": 1.0 + nrm(ks[13], (L, KV_LORA), 0.02),
        "w_ukv": nrm(ks[14], (L, KV_LORA, MLA_HEADS * (QK_NOPE + V_HEAD)), KV_LORA ** -0.5),
        "q_norm_g": 1.0 + nrm(ks[15], (L, QK_HEAD), 0.02),
        "k_norm_g": 1.0 + nrm(ks[16], (L, QK_HEAD), 0.02),
        "w_out": nrm(ks[17], (L, MIX_WIDTH, D), MIX_WIDTH ** -0.5),
        "norm2_g": 1.0 + nrm(ks[18], (L, D), 0.02),
        "peer_wq": nrm(ks[19], (L, D, PEER_HEADS * PEER_QDIM), D ** -0.5),
        "peer_k1": nrm(ks[20], (L, PEER_HEADS, N_KEYS, PEER_HALF), PEER_HALF ** -0.5),
        "peer_k2": nrm(ks[21], (L, PEER_HEADS, N_KEYS, PEER_HALF), PEER_HALF ** -0.5),
        "peer_u": nrm(ks[22], (L, N_EXPERTS, D), D ** -0.5),
        "peer_v": nrm(ks[23], (L, N_EXPERTS, D), 0.3),
    }


def reference(x, c, positions, w_ada, b_ada, norm1_g, w_in, conv_dw_w, conv_dw_b, conv_ln_g,
              conv_ln_b, q_a_norm_g, w_uq, kv_a_norm_g, w_ukv, q_norm_g, k_norm_g, w_out,
              norm2_g, peer_wq, peer_k1, peer_k2, peer_u, peer_v):
    c_act = jax.nn.silu(c)
    for l in range(DEPTH):
        mod = c_act @ w_ada[l] + b_ada[l]
        shift1, scale1, gate1, shift2, scale2, gate2 = [m[:, None, :] for m in jnp.split(mod, 6, axis=-1)]

        h = rmsnorm(x, norm1_g[l]) * (1.0 + scale1) + shift1
        a = h @ w_in[l]
        o1 = 2 * CONV_CH
        o2 = o1 + Q_LORA
        o3 = o2 + KV_LORA
        conv_out = conv_module(a[..., :o1], conv_dw_w[l], conv_dw_b[l], conv_ln_g[l], conv_ln_b[l])
        attn_out = mla(a[..., o1:o2], a[..., o2:o3], a[..., o3:], positions,
                       q_a_norm_g[l], w_uq[l], kv_a_norm_g[l], w_ukv[l], q_norm_g[l], k_norm_g[l])
        mix = jnp.concatenate([conv_out, attn_out], axis=-1) @ w_out[l]
        x = x + gate1 * mix

        h2 = rmsnorm(x, norm2_g[l]) * (1.0 + scale2) + shift2
        x = x + gate2 * peer(h2, peer_wq[l], peer_k1[l], peer_k2[l], peer_u[l], peer_v[l])
    return x
```

```python
import functools

import numpy as np
import jax
import jax.numpy as jnp
from jax import lax
from jax.experimental import pallas as pl
from jax.experimental.pallas import tpu as pltpu

F32 = jnp.float32
BF16 = jnp.bfloat16

CHUNK = 64
EPS = 1e-6
CONV_CH = 1024
CONV_WIDTH = 31
MLA_HEADS = 8
QK_NOPE = 128
QK_ROPE = 64
QK_HEAD = QK_NOPE + QK_ROPE
V_HEAD = 128
Q_LORA = 768
KV_LORA = 512
ROPE_THETA = 10000.0
PEER_HEADS = 8
N_KEYS = 128
PEER_TOPK = 16
PEER_HALF = 128

LANES = 128
HEAD_PAD = 256
CONV_HALO = 32
VMEM_LIMIT = 56 * 1024 * 1024

NEG_INF = float("-inf")


def _cparams(n_axes, vmem=VMEM_LIMIT):
    return pltpu.CompilerParams(dimension_semantics=("arbitrary",) * n_axes, vmem_limit_bytes=vmem)


def _nt_dot(a, b):
    return lax.dot_general(a, b, (((1,), (1,)), ((), ())), preferred_element_type=F32)


def _tn_dot(a, b):
    return lax.dot_general(a, b, (((0,), (0,)), ((), ())), preferred_element_type=F32)


def _ada_kernel(c_ref, w_ref, b_ref, o_ref):
    c = c_ref[...]
    ca = (c * jax.nn.sigmoid(c)).astype(BF16)
    o_ref[...] = jnp.dot(ca, w_ref[...].astype(BF16), preferred_element_type=F32) + b_ref[...]


def _ada_mod(c, w_ada, b_ada):
    d = c.shape[1]
    n = w_ada.shape[1]
    tn = 1024
    c8 = jnp.broadcast_to(c, (8, d))
    out = pl.pallas_call(
        _ada_kernel,
        grid=(n // tn,),
        in_specs=[pl.BlockSpec((8, d), lambda j: (0, 0)),
                  pl.BlockSpec((d, tn), lambda j: (0, j)),
                  pl.BlockSpec((1, tn), lambda j: (0, j))],
        out_specs=pl.BlockSpec((8, tn), lambda j: (0, j)),
        out_shape=jax.ShapeDtypeStruct((8, n), F32),
        compiler_params=_cparams(1),
        name="ada_mod",
    )(c8, w_ada, b_ada.reshape(1, n))
    return out[0:1]


def _rope_table_kernel(pos_ref, invf_ref, cos_ref, sin_ref):
    ang = pos_ref[...] * invf_ref[...]
    cos_ref[...] = jnp.cos(ang)
    sin_ref[...] = jnp.sin(ang)


def _rope_tables(positions):
    s = positions.shape[0]
    half = QK_ROPE // 2
    rep = LANES // half
    inv_freq = ROPE_THETA ** (-jnp.arange(0, QK_ROPE, 2, dtype=F32) / QK_ROPE)
    pos_rep = jnp.repeat(positions.astype(F32), half).reshape(s // rep, LANES)
    invf = jnp.tile(inv_freq, rep).reshape(1, LANES)
    cos_d, sin_d = pl.pallas_call(
        _rope_table_kernel,
        out_shape=(jax.ShapeDtypeStruct((s // rep, LANES), F32),) * 2,
        name="rope_table",
    )(pos_rep, invf)
    cos = cos_d.reshape(s, half)
    sin = sin_d.reshape(s, half)
    zeros = jnp.zeros((s, LANES - QK_ROPE), F32)
    return (jnp.concatenate([cos, cos, zeros], axis=1),
            jnp.concatenate([-sin, sin, zeros], axis=1))


def _inproj_kernel(x_ref, g_ref, sc_ref, sh_ref, w_ref, conv_ref, ql_ref, kvl_ref, kr_ref):
    x = x_ref[...]
    ms = jnp.mean(x * x, axis=-1, keepdims=True)
    h = x * lax.rsqrt(ms + EPS) * g_ref[...]
    h = h * (1.0 + sc_ref[...]) + sh_ref[...]
    a = jnp.dot(h.astype(BF16), w_ref[...], preferred_element_type=F32)
    o1 = 2 * CONV_CH
    o2 = o1 + Q_LORA
    o3 = o2 + KV_LORA
    conv_ref[...] = a[:, :o1]
    ql_ref[...] = a[:, o1:o2]
    kvl_ref[...] = a[:, o2:o3]
    kr_ref[...] = a[:, o3:o3 + LANES]


def _in_proj(x, g, scale, shift, w_in_pad, tm):
    s, d = x.shape
    n = w_in_pad.shape[1]
    row = lambda i: (i, 0)
    fixed = lambda i: (0, 0)
    return pl.pallas_call(
        _inproj_kernel,
        grid=(s // tm,),
        in_specs=[pl.BlockSpec((tm, d), row),
                  pl.BlockSpec((1, d), fixed), pl.BlockSpec((1, d), fixed), pl.BlockSpec((1, d), fixed),
                  pl.BlockSpec((d, n), fixed, pipeline_mode=pl.Buffered(1))],
        out_specs=[pl.BlockSpec((tm, 2 * CONV_CH), row), pl.BlockSpec((tm, Q_LORA), row),
                   pl.BlockSpec((tm, KV_LORA), row), pl.BlockSpec((tm, LANES), row)],
        out_shape=[jax.ShapeDtypeStruct((s, 2 * CONV_CH), F32), jax.ShapeDtypeStruct((s, Q_LORA), F32),
                   jax.ShapeDtypeStruct((s, KV_LORA), F32), jax.ShapeDtypeStruct((s, LANES), F32)],
        compiler_params=_cparams(1),
        name="in_proj",
    )(x, g, scale, shift, w_in_pad)


def _conv_kernel(cur_ref, halo_ref, w_ref, b_ref, g_ref, beta_ref, o_ref, hbuf, *, tm, rc):
    i = pl.program_id(0)
    cur = cur_ref[...]
    hbuf[pl.ds(CONV_HALO, tm), :] = cur[:, :CONV_CH] * jax.nn.sigmoid(cur[:, CONV_CH:])
    hal = halo_ref[...]
    hg = hal[:, :CONV_CH] * jax.nn.sigmoid(hal[:, CONV_CH:])
    hbuf[pl.ds(0, CONV_HALO), :] = jnp.where(i > 0, hg, 0.0)
    first = CONV_HALO - (CONV_WIDTH - 1)
    for r0 in range(0, tm, rc):
        acc = jnp.zeros((rc, CONV_CH), F32)
        for k in range(CONV_WIDTH):
            acc = acc + w_ref[pl.ds(k, 1), :] * hbuf[pl.ds(r0 + first + k, rc), :]
        acc = acc + b_ref[...]
        mu = jnp.mean(acc, axis=-1, keepdims=True)
        cen = acc - mu
        var = jnp.mean(cen * cen, axis=-1, keepdims=True)
        y = cen * lax.rsqrt(var + EPS) * g_ref[...] + beta_ref[...]
        o_ref[pl.ds(r0, rc), :] = (y * jax.nn.sigmoid(y)).astype(BF16)


def _conv_module(a_conv, dw_w, dw_b, ln_g, ln_b, tm):
    s = a_conv.shape[0]
    hb = tm // CONV_HALO
    fixed = lambda i: (0, 0)
    return pl.pallas_call(
        functools.partial(_conv_kernel, tm=tm, rc=32),
        grid=(s // tm,),
        in_specs=[pl.BlockSpec((tm, 2 * CONV_CH), lambda i: (i, 0)),
                  pl.BlockSpec((CONV_HALO, 2 * CONV_CH), lambda i: (jnp.maximum(i * hb - 1, 0), 0)),
                  pl.BlockSpec((CONV_WIDTH, CONV_CH), fixed),
                  pl.BlockSpec((1, CONV_CH), fixed), pl.BlockSpec((1, CONV_CH), fixed),
                  pl.BlockSpec((1, CONV_CH), fixed)],
        out_specs=pl.BlockSpec((tm, CONV_CH), lambda i: (i, 0)),
        out_shape=jax.ShapeDtypeStruct((s, CONV_CH), BF16),
        scratch_shapes=[pltpu.VMEM((tm + CONV_HALO, CONV_CH), F32)],
        compiler_params=_cparams(1),
        name="conv_module",
    )(a_conv, a_conv, dw_w, dw_b, ln_g, ln_b)


def _mla_proj_kernel(ql_ref, kvl_ref, kr_ref, cos_ref, sin_ref, gqa_ref, gkva_ref, wuq_ref, wukv_ref,
                     gq_ref, gk_ref, q_out, k_out, vt_out):
    def rms(v, g):
        return v * lax.rsqrt(jnp.mean(v * v, axis=-1, keepdims=True) + EPS) * g

    q = jnp.dot(rms(ql_ref[...], gqa_ref[...]).astype(BF16), wuq_ref[...], preferred_element_type=F32)
    kv = jnp.dot(rms(kvl_ref[...], gkva_ref[...]).astype(BF16), wukv_ref[...], preferred_element_type=F32)
    kr = kr_ref[...]
    cos = cos_ref[...]
    sin = sin_ref[...]
    lane = lax.broadcasted_iota(jnp.int32, kr.shape, 1)
    half = QK_ROPE // 2

    def rope(v):
        rot = jnp.where(lane < half, pltpu.roll(v, LANES - half, 1), pltpu.roll(v, half, 1))
        return v * cos + rot * sin

    gq = gq_ref[...]
    gk = gk_ref[...]
    scale = QK_HEAD ** -0.5
    ss_kr = jnp.sum(kr * kr, axis=-1, keepdims=True)
    for h in range(MLA_HEADS):
        qh = q[:, h * HEAD_PAD:(h + 1) * HEAD_PAD]
        rq = lax.rsqrt(jnp.sum(qh * qh, axis=-1, keepdims=True) / QK_HEAD + EPS)
        qh = qh * rq * gq
        q_out[h, :, 0:QK_NOPE] = (qh[:, :QK_NOPE] * scale).astype(BF16)
        q_out[h, :, QK_NOPE:HEAD_PAD] = (rope(qh[:, QK_NOPE:]) * scale).astype(BF16)
        kn = kv[:, h * HEAD_PAD:h * HEAD_PAD + QK_NOPE]
        vh = kv[:, h * HEAD_PAD + QK_NOPE:(h + 1) * HEAD_PAD]
        rk = lax.rsqrt((jnp.sum(kn * kn, axis=-1, keepdims=True) + ss_kr) / QK_HEAD + EPS)
        k_out[h, :, 0:QK_NOPE] = (kn * rk * gk[:, :QK_NOPE]).astype(BF16)
        k_out[h, :, QK_NOPE:HEAD_PAD] = rope(kr * rk * gk[:, QK_NOPE:]).astype(BF16)
        vt_out[h] = vh.T.astype(BF16)


def _mla_proj(q_lat, kv_lat, k_rope, cos_t, sin_t, gqa, gkva, wuq_pad, wukv, gq_pad, gk_pad, tm):
    s = q_lat.shape[0]
    row = lambda i: (i, 0)
    fixed = lambda i: (0, 0)
    hq = MLA_HEADS * HEAD_PAD
    return pl.pallas_call(
        _mla_proj_kernel,
        grid=(s // tm,),
        in_specs=[pl.BlockSpec((tm, Q_LORA), row), pl.BlockSpec((tm, KV_LORA), row),
                  pl.BlockSpec((tm, LANES), row), pl.BlockSpec((tm, LANES), row), pl.BlockSpec((tm, LANES), row),
                  pl.BlockSpec((1, Q_LORA), fixed), pl.BlockSpec((1, KV_LORA), fixed),
                  pl.BlockSpec((Q_LORA, hq), fixed), pl.BlockSpec((KV_LORA, hq), fixed),
                  pl.BlockSpec((1, HEAD_PAD), fixed), pl.BlockSpec((1, HEAD_PAD), fixed)],
        out_specs=[pl.BlockSpec((MLA_HEADS, tm, HEAD_PAD), lambda i: (0, i, 0)),
                   pl.BlockSpec((MLA_HEADS, tm, HEAD_PAD), lambda i: (0, i, 0)),
                   pl.BlockSpec((MLA_HEADS, V_HEAD, tm), lambda i: (0, 0, i))],
        out_shape=[jax.ShapeDtypeStruct((MLA_HEADS, s, HEAD_PAD), BF16),
                   jax.ShapeDtypeStruct((MLA_HEADS, s, HEAD_PAD), BF16),
                   jax.ShapeDtypeStruct((MLA_HEADS, V_HEAD, s), BF16)],
        compiler_params=_cparams(1),
        name="mla_proj",
    )(q_lat, kv_lat, k_rope, cos_t, sin_t, gqa, gkva, wuq_pad, wukv, gq_pad, gk_pad)


def _attn_kernel(q_ref, k_ref, vt_ref, o_ref, *, tq):
    qi = pl.program_id(1)
    q = q_ref[0]

    def step(kt, carry, masked):
        m, l, acc = carry
        k0 = pl.multiple_of(kt * tq, tq)
        s = _nt_dot(k_ref[0, pl.ds(k0, tq), :], q)
        if masked:
            kc = (k0 + lax.broadcasted_iota(jnp.int32, s.shape, 0)) // CHUNK
            qc = (qi * tq + lax.broadcasted_iota(jnp.int32, s.shape, 1)) // CHUNK
            s = jnp.where(kc <= qc, s, NEG_INF)
        m_new = jnp.maximum(m, jnp.max(s, axis=0, keepdims=True))
        alpha = jnp.exp(m - m_new)
        p = jnp.exp(s - m_new)
        l = alpha * l + jnp.sum(p, axis=0, keepdims=True)
        acc = alpha * acc + jnp.dot(vt_ref[0, :, pl.ds(k0, tq)], p.astype(BF16), preferred_element_type=F32)
        return m_new, l, acc

    init = (jnp.full((1, tq), NEG_INF, F32), jnp.zeros((1, tq), F32), jnp.zeros((V_HEAD, tq), F32))
    carry = lax.fori_loop(0, qi, lambda kt, c: step(kt, c, False), init)
    _, l, acc = step(qi, carry, True)
    o_ref[...] = (acc / l).T.astype(BF16)


def _attention(q, k, vt, tq):
    h, s, _ = q.shape
    return pl.pallas_call(
        functools.partial(_attn_kernel, tq=tq),
        grid=(h, s // tq),
        in_specs=[pl.BlockSpec((1, tq, HEAD_PAD), lambda hh, i: (hh, i, 0)),
                  pl.BlockSpec((1, s, HEAD_PAD), lambda hh, i: (hh, 0, 0)),
                  pl.BlockSpec((1, V_HEAD, s), lambda hh, i: (hh, 0, 0))],
        out_specs=pl.BlockSpec((tq, V_HEAD), lambda hh, i: (i, hh)),
        out_shape=jax.ShapeDtypeStruct((s, h * V_HEAD), BF16),
        compiler_params=_cparams(2),
        name="attention",
    )(q, k, vt)


def _outproj_kernel(x_ref, conv_ref, attn_ref, w_ref, gate_ref, g_ref, sc_ref, sh_ref, x1_ref, h2_ref):
    mix = jnp.dot(conv_ref[...], w_ref[0:CONV_CH, :], preferred_element_type=F32)
    mix = mix + jnp.dot(attn_ref[...], w_ref[CONV_CH:, :], preferred_element_type=F32)
    x1 = x_ref[...] + gate_ref[...] * mix
    x1_ref[...] = x1
    ms = jnp.mean(x1 * x1, axis=-1, keepdims=True)
    h2 = x1 * lax.rsqrt(ms + EPS) * g_ref[...]
    h2_ref[...] = (h2 * (1.0 + sc_ref[...]) + sh_ref[...]).astype(BF16)


def _out_proj(x, conv_out, attn_out, w_out, gate1, g2, scale2, shift2, tm):
    s, d = x.shape
    kdim = w_out.shape[0]
    row = lambda i: (i, 0)
    fixed = lambda i: (0, 0)
    return pl.pallas_call(
        _outproj_kernel,
        grid=(s // tm,),
        in_specs=[pl.BlockSpec((tm, d), row), pl.BlockSpec((tm, CONV_CH), row),
                  pl.BlockSpec((tm, kdim - CONV_CH), row),
                  pl.BlockSpec((kdim, d), fixed, pipeline_mode=pl.Buffered(1)),
                  pl.BlockSpec((1, d), fixed), pl.BlockSpec((1, d), fixed),
                  pl.BlockSpec((1, d), fixed), pl.BlockSpec((1, d), fixed)],
        out_specs=[pl.BlockSpec((tm, d), row), pl.BlockSpec((tm, d), row)],
        out_shape=[jax.ShapeDtypeStruct((s, d), F32), jax.ShapeDtypeStruct((s, d), BF16)],
        compiler_params=_cparams(1),
        name="out_proj",
    )(x, conv_out, attn_out, w_out, gate1, g2, scale2, shift2)


def _peer_scores_kernel(h2_ref, wq_ref, k1_ref, k2_ref, s1_ref, s2_ref):
    q = jnp.dot(h2_ref[...], wq_ref[...], preferred_element_type=F32).astype(BF16)
    for h in range(PEER_HEADS):
        base = h * 2 * PEER_HALF
        s1_ref[h] = _nt_dot(k1_ref[h], q[:, base:base + PEER_HALF])
        s2_ref[h] = _nt_dot(k2_ref[h], q[:, base + PEER_HALF:base + 2 * PEER_HALF])


def _peer_scores(h2, wq, k1, k2, tm):
    t, d = h2.shape
    spec = pl.BlockSpec((PEER_HEADS, N_KEYS, tm), lambda i: (0, 0, i))
    keys = pl.BlockSpec((PEER_HEADS, N_KEYS, PEER_HALF), lambda i: (0, 0, 0))
    shape = jax.ShapeDtypeStruct((PEER_HEADS, N_KEYS, t), F32)
    return pl.pallas_call(
        _peer_scores_kernel,
        grid=(t // tm,),
        in_specs=[pl.BlockSpec((tm, d), lambda i: (i, 0)),
                  pl.BlockSpec(wq.shape, lambda i: (0, 0), pipeline_mode=pl.Buffered(1)),
                  keys, keys],
        out_specs=[spec, spec],
        out_shape=[shape, shape],
        compiler_params=_cparams(1),
        name="peer_scores",
    )(h2, wq, k1, k2)


_CAND_ROWS = 80


def _candidate_table():
    tbl = np.full((_CAND_ROWS,), -1, np.int32)
    for i in range(PEER_TOPK):
        tbl[i] = i * PEER_TOPK
    for j in range(1, 8):
        for i in range(8):
            if (i + 1) * (j + 1) <= PEER_TOPK:
                tbl[16 + 8 * (j - 1) + i] = i * PEER_TOPK + j
    for j in range(8, PEER_TOPK):
        tbl[72 + j - 8] = j
    return np.broadcast_to(tbl[:, None], (_CAND_ROWS, LANES)).copy()


def _top16_rows(s):
    row = lax.broadcasted_iota(jnp.int32, s.shape, 0)
    rank = jnp.full(s.shape, float(PEER_TOPK), F32)
    vals = []
    for r in range(PEER_TOPK):
        m = jnp.max(s, axis=0, keepdims=True)
        first = jnp.min(jnp.where(s == m, row, N_KEYS), axis=0, keepdims=True)
        sel = row == first
        rank = jnp.where(sel, float(r), rank)
        s = jnp.where(sel, NEG_INF, s)
        vals.append(m)
    return jnp.concatenate(vals, axis=0), rank


def _peer_select_kernel(s1_ref, s2_ref, tbl_ref, n1_ref, a1_ref, r2_ref, a2_ref, *, chunks):
    flat = tbl_ref[...]
    valid = flat >= 0
    row8 = lax.broadcasted_iota(jnp.int32, (8, LANES), 0)
    def chunk(c, _):
        sl = pl.ds(pl.multiple_of(c * LANES, LANES), LANES)
        s1 = s1_ref[0, :, sl]
        s2 = s2_ref[0, :, sl]
        v1, rank1 = _top16_rows(s1)
        v2, rank2 = _top16_rows(s2)
        blocks = [v1 + v2[0:1]]
        for j in range(1, 8):
            blocks.append(v1[0:8] + v2[j:j + 1])
        blocks.append(v1[0:1] + v2[8:16])
        cand0 = jnp.where(valid, jnp.concatenate(blocks, axis=0), NEG_INF)
        cand = cand0
        chosen = jnp.zeros(cand.shape, F32)
        for _ in range(PEER_TOPK):
            m = jnp.max(cand, axis=0, keepdims=True)
            first = jnp.min(jnp.where(cand == m, flat, PEER_TOPK * PEER_TOPK), axis=0, keepdims=True)
            sel = flat == first
            chosen = jnp.where(sel, 1.0, chosen)
            cand = jnp.where(sel, NEG_INF, cand)
        n_lo = chosen[0:8]
        for j in range(1, 8):
            n_lo = n_lo + chosen[16 + 8 * (j - 1):16 + 8 * j]
        tail = jnp.sum(chosen[72:80], axis=0, keepdims=True)
        n_lo = n_lo + jnp.where(row8 == 0, tail, 0.0)
        n = jnp.concatenate([n_lo, chosen[8:16]], axis=0)
        top = v1[0:1] + v2[0:1]
        z = jnp.sum(jnp.where(chosen > 0.0, jnp.exp(cand0 - top), 0.0), axis=0, keepdims=True)
        n1 = jnp.zeros(s1.shape, F32)
        for r in range(PEER_TOPK):
            n1 = jnp.where(rank1 == float(r), n[r:r + 1], n1)
        n1_ref[0, :, sl] = n1
        a1_ref[0, :, sl] = jnp.exp(s1 - v1[0:1])
        r2_ref[0, :, sl] = rank2
        a2_ref[0, :, sl] = jnp.exp(s2 - v2[0:1]) / z
        return 0

    lax.fori_loop(0, chunks, chunk, 0)


def _peer_select(s1, s2, tl):
    h, n, t = s1.shape
    spec = pl.BlockSpec((1, n, tl), lambda hh, i: (hh, 0, i))
    shape = jax.ShapeDtypeStruct((h, n, t), F32)
    tbl = jnp.asarray(_candidate_table())
    return pl.pallas_call(
        functools.partial(_peer_select_kernel, chunks=tl // LANES),
        grid=(h, t // tl),
        in_specs=[spec, spec, pl.BlockSpec((_CAND_ROWS, LANES), lambda hh, i: (0, 0))],
        out_specs=[spec] * 4,
        out_shape=[shape] * 4,
        compiler_params=_cparams(2),
        name="peer_select",
    )(s1, s2, tbl)


def _gelu_tanh(x):
    return 0.5 * x * (1.0 + jnp.tanh(0.7978845608028654 * (x + 0.044715 * (x * x * x))))


def _peer_dense_kernel(h2_ref, u_ref, v_ref, n1_ref, a1_ref, r2_ref, a2_ref, x1_ref, gate_ref, o_ref,
                       act_ref, *, te):
    j = pl.program_id(1)

    @pl.when(j == 0)
    def _():
        o_ref[...] = jnp.zeros_like(o_ref)

    a_t = _nt_dot(u_ref[...], h2_ref[...])
    for e in range(te // N_KEYS):
        rows = pl.ds(e * N_KEYS, N_KEYS)
        gate = jnp.zeros((N_KEYS, a_t.shape[1]), F32)
        for h in range(PEER_HEADS):
            n1 = n1_ref[h, pl.ds(e, 1), :]
            a1 = a1_ref[h, pl.ds(e, 1), :]
            gate = gate + jnp.where(r2_ref[h] < n1, a1 * a2_ref[h], 0.0)
        act_ref[rows, :] = (gate * _gelu_tanh(a_t[e * N_KEYS:(e + 1) * N_KEYS, :])).astype(BF16)
    o_ref[...] += _tn_dot(act_ref[...], v_ref[...])

    @pl.when(j == pl.num_programs(1) - 1)
    def _():
        o_ref[...] = x1_ref[...] + gate_ref[...] * o_ref[...]


def _peer_dense(h2, u, v, n1, a1, r2, a2, x1, gate2, tm, te):
    t, d = h2.shape
    n_exp = u.shape[0]
    e1_per_tile = te // N_KEYS
    small = pl.BlockSpec((PEER_HEADS, e1_per_tile, tm), lambda i, j: (0, j, i))
    big = pl.BlockSpec((PEER_HEADS, N_KEYS, tm), lambda i, j: (0, 0, i), pipeline_mode=pl.Buffered(1))
    return pl.pallas_call(
        functools.partial(_peer_dense_kernel, te=te),
        grid=(t // tm, n_exp // te),
        in_specs=[pl.BlockSpec((tm, d), lambda i, j: (i, 0), pipeline_mode=pl.Buffered(1)),
                  pl.BlockSpec((te, d), lambda i, j: (j, 0)),
                  pl.BlockSpec((te, d), lambda i, j: (j, 0)),
                  small, small, big, big,
                  pl.BlockSpec((tm, d), lambda i, j: (i, 0), pipeline_mode=pl.Buffered(1)),
                  pl.BlockSpec((1, d), lambda i, j: (0, 0))],
        out_specs=pl.BlockSpec((tm, d), lambda i, j: (i, 0)),
        out_shape=jax.ShapeDtypeStruct((t, d), F32),
        scratch_shapes=[pltpu.VMEM((te, tm), BF16)],
        compiler_params=_cparams(2),
        name="peer_dense",
    )(h2, u, v, n1, a1, r2, a2, x1, gate2)


def _pad_cols(w, width):
    return jnp.pad(w, ((0, 0), (0, width - w.shape[1])))


def _head_pad_cols(w, per_head):
    k = w.shape[0]
    w = w.reshape(k, MLA_HEADS, per_head)
    w = jnp.pad(w, ((0, 0), (0, 0), (0, HEAD_PAD - per_head)))
    return w.reshape(k, MLA_HEADS * HEAD_PAD)


def _layer(x, mod, positions_tables, p, tiles):
    d = x.shape[1]
    shift1, scale1, gate1, shift2, scale2, gate2 = [mod[:, i * d:(i + 1) * d] for i in range(6)]
    cos_t, sin_t = positions_tables

    w_in = _pad_cols(p["w_in"], 2 * CONV_CH + Q_LORA + KV_LORA + LANES).astype(BF16)
    a_conv, q_lat, kv_lat, k_rope = _in_proj(x, p["norm1_g"], scale1, shift1, w_in, tiles["proj"])

    conv_out = _conv_module(a_conv, p["conv_dw_w"], p["conv_dw_b"], p["conv_ln_g"], p["conv_ln_b"],
                            tiles["conv"])

    wuq = _head_pad_cols(p["w_uq"], QK_HEAD).astype(BF16)
    gq = _pad_cols(p["q_norm_g"], HEAD_PAD)
    gk = _pad_cols(p["k_norm_g"], HEAD_PAD)
    q, k, vt = _mla_proj(q_lat, kv_lat, k_rope, cos_t, sin_t, p["q_a_norm_g"], p["kv_a_norm_g"],
                         wuq, p["w_ukv"].astype(BF16), gq, gk, tiles["proj"])
    attn_out = _attention(q, k, vt, tiles["attn"])

    x1, h2 = _out_proj(x, conv_out, attn_out, p["w_out"].astype(BF16), gate1, p["norm2_g"], scale2, shift2,
                       tiles["proj"])

    s1, s2 = _peer_scores(h2, p["peer_wq"].astype(BF16), p["peer_k1"].astype(BF16),
                          p["peer_k2"].astype(BF16), tiles["proj"])
    n1, a1, r2, a2 = _peer_select(s1, s2, tiles["select"])
    return _peer_dense(h2, p["peer_u"].astype(BF16), p["peer_v"].astype(BF16), n1, a1, r2, a2, x1, gate2,
                       tiles["dense_tokens"], tiles["dense_experts"])


def _tiles(s):
    return {"proj": min(512, s), "conv": min(256, s), "attn": min(512, s), "select": min(1024, s),
            "dense_tokens": min(512, s), "dense_experts": 1024}


def kernel(x, c, positions, w_ada, b_ada, norm1_g, w_in, conv_dw_w, conv_dw_b, conv_ln_g, conv_ln_b,
           q_a_norm_g, w_uq, kv_a_norm_g, w_ukv, q_norm_g, k_norm_g, w_out, norm2_g, peer_wq, peer_k1,
           peer_k2, peer_u, peer_v):
    b, s, d = x.shape
    assert b == 1, "kernel is written for batch 1"
    depth = w_ada.shape[0]
    row = lambda a, l: a[l].reshape(1, -1)
    tables = _rope_tables(positions[0])
    xs = x[0]
    for l in range(depth):
        mod = _ada_mod(c, w_ada[l], b_ada[l])
        params = {
            "norm1_g": row(norm1_g, l), "w_in": w_in[l], "conv_dw_w": conv_dw_w[l],
            "conv_dw_b": row(conv_dw_b, l), "conv_ln_g": row(conv_ln_g, l), "conv_ln_b": row(conv_ln_b, l),
            "q_a_norm_g": row(q_a_norm_g, l), "w_uq": w_uq[l], "kv_a_norm_g": row(kv_a_norm_g, l),
            "w_ukv": w_ukv[l], "q_norm_g": row(q_norm_g, l), "k_norm_g": row(k_norm_g, l),
            "w_out": w_out[l], "norm2_g": row(norm2_g, l), "peer_wq": peer_wq[l],
            "peer_k1": peer_k1[l], "peer_k2": peer_k2[l], "peer_u": peer_u[l], "peer_v": peer_v[l],
        }
        xs = _layer(xs, mod, tables, params, _tiles(s))
    return xs[None]
```

```python
import functools

import numpy as np
import jax
import jax.numpy as jnp
from jax import lax
from jax.experimental import pallas as pl
from jax.experimental.pallas import tpu as pltpu

F32 = jnp.float32
BF16 = jnp.bfloat16

CHUNK = 64
EPS = 1e-6
CONV_CH = 1024
CONV_WIDTH = 31
MLA_HEADS = 8
QK_NOPE = 128
QK_ROPE = 64
QK_HEAD = QK_NOPE + QK_ROPE
V_HEAD = 128
Q_LORA = 768
KV_LORA = 512
ROPE_THETA = 10000.0
PEER_HEADS = 8
N_KEYS = 128
PEER_TOPK = 16
PEER_HALF = 128

LANES = 128
HEAD_PAD = 256
CONV_HALO = 32
VMEM_LIMIT = 56 * 1024 * 1024

RANK_SCALE = 1024.0
NEG_INF = float("-inf")
LOG2_E = 1.4426950408889634


def _cparams(n_axes, vmem=VMEM_LIMIT, flags=None):
    return pltpu.CompilerParams(dimension_semantics=("arbitrary",) * n_axes, vmem_limit_bytes=vmem,
                                flags=flags)


def _nt_dot(a, b):
    return lax.dot_general(a, b, (((1,), (1,)), ((), ())), preferred_element_type=F32)


def _tn_dot(a, b):
    return lax.dot_general(a, b, (((0,), (0,)), ((), ())), preferred_element_type=F32)


def _ada_kernel(c_ref, w_ref, b_ref, o_ref):
    c = c_ref[...]
    ca = (c * jax.nn.sigmoid(c)).astype(BF16)
    o_ref[...] = jnp.dot(ca, w_ref[...].astype(BF16), preferred_element_type=F32) + b_ref[...]


def _ada_mod(c, w_ada, b_ada):
    d = c.shape[1]
    n = w_ada.shape[1]
    tn = 1024
    c8 = jnp.broadcast_to(c, (8, d))
    out = pl.pallas_call(
        _ada_kernel,
        grid=(n // tn,),
        in_specs=[pl.BlockSpec((8, d), lambda j: (0, 0)),
                  pl.BlockSpec((d, tn), lambda j: (0, j)),
                  pl.BlockSpec((1, tn), lambda j: (0, j))],
        out_specs=pl.BlockSpec((8, tn), lambda j: (0, j)),
        out_shape=jax.ShapeDtypeStruct((8, n), F32),
        compiler_params=_cparams(1),
        name="ada_mod",
    )(c8, w_ada, b_ada.reshape(1, n))
    return out[0:1]


def _rope_table_kernel(pos_ref, invf_ref, cos_ref, sin_ref):
    ang = pos_ref[...] * invf_ref[...]
    cos_ref[...] = jnp.cos(ang)
    sin_ref[...] = jnp.sin(ang)


def _rope_tables(positions):
    s = positions.shape[0]
    half = QK_ROPE // 2
    rep = LANES // half
    inv_freq = ROPE_THETA ** (-jnp.arange(0, QK_ROPE, 2, dtype=F32) / QK_ROPE)
    pos_rep = jnp.repeat(positions.astype(F32), half).reshape(s // rep, LANES)
    invf = jnp.tile(inv_freq, rep).reshape(1, LANES)
    cos_d, sin_d = pl.pallas_call(
        _rope_table_kernel,
        out_shape=(jax.ShapeDtypeStruct((s // rep, LANES), F32),) * 2,
        name="rope_table",
    )(pos_rep, invf)
    cos = cos_d.reshape(s, half)
    sin = sin_d.reshape(s, half)
    zeros = jnp.zeros((s, LANES - QK_ROPE), F32)
    return (jnp.concatenate([cos, cos, zeros], axis=1),
            jnp.concatenate([-sin, sin, zeros], axis=1))


def _inproj_kernel(x_ref, g_ref, sc_ref, sh_ref, w_ref, conv_ref, ql_ref, kvl_ref, kr_ref):
    x = x_ref[...]
    ms = jnp.mean(x * x, axis=-1, keepdims=True)
    h = x * lax.rsqrt(ms + EPS) * g_ref[...]
    h = h * (1.0 + sc_ref[...]) + sh_ref[...]
    a = jnp.dot(h.astype(BF16), w_ref[...], preferred_element_type=F32)
    o1 = 2 * CONV_CH
    o2 = o1 + Q_LORA
    o3 = o2 + KV_LORA
    conv_ref[...] = a[:, :o1]
    ql_ref[...] = a[:, o1:o2]
    kvl_ref[...] = a[:, o2:o3]
    kr_ref[...] = a[:, o3:o3 + LANES]


def _in_proj(x, g, scale, shift, w_in_pad, tm):
    s, d = x.shape
    n = w_in_pad.shape[1]
    row = lambda i: (i, 0)
    fixed = lambda i: (0, 0)
    return pl.pallas_call(
        _inproj_kernel,
        grid=(s // tm,),
        in_specs=[pl.BlockSpec((tm, d), row),
                  pl.BlockSpec((1, d), fixed), pl.BlockSpec((1, d), fixed), pl.BlockSpec((1, d), fixed),
                  pl.BlockSpec((d, n), fixed, pipeline_mode=pl.Buffered(1))],
        out_specs=[pl.BlockSpec((tm, 2 * CONV_CH), row), pl.BlockSpec((tm, Q_LORA), row),
                   pl.BlockSpec((tm, KV_LORA), row), pl.BlockSpec((tm, LANES), row)],
        out_shape=[jax.ShapeDtypeStruct((s, 2 * CONV_CH), F32), jax.ShapeDtypeStruct((s, Q_LORA), F32),
                   jax.ShapeDtypeStruct((s, KV_LORA), F32), jax.ShapeDtypeStruct((s, LANES), F32)],
        compiler_params=_cparams(1),
        name="in_proj",
    )(x, g, scale, shift, w_in_pad)


def _conv_kernel(cur_ref, halo_ref, w_ref, b_ref, g_ref, beta_ref, o_ref, hbuf, *, tm, rc):
    i = pl.program_id(0)
    cur = cur_ref[...]
    hbuf[pl.ds(CONV_HALO, tm), :] = cur[:, :CONV_CH] * jax.nn.sigmoid(cur[:, CONV_CH:])
    hal = halo_ref[...]
    hg = hal[:, :CONV_CH] * jax.nn.sigmoid(hal[:, CONV_CH:])
    hbuf[pl.ds(0, CONV_HALO), :] = jnp.where(i > 0, hg, 0.0)
    first = CONV_HALO - (CONV_WIDTH - 1)
    for r0 in range(0, tm, rc):
        acc = jnp.zeros((rc, CONV_CH), F32)
        for k in range(CONV_WIDTH):
            acc = acc + w_ref[pl.ds(k, 1), :] * hbuf[pl.ds(r0 + first + k, rc), :]
        acc = acc + b_ref[...]
        mu = jnp.mean(acc, axis=-1, keepdims=True)
        cen = acc - mu
        var = jnp.mean(cen * cen, axis=-1, keepdims=True)
        y = cen * lax.rsqrt(var + EPS) * g_ref[...] + beta_ref[...]
        o_ref[pl.ds(r0, rc), :] = (y * jax.nn.sigmoid(y)).astype(BF16)


def _conv_module(a_conv, dw_w, dw_b, ln_g, ln_b, tm):
    s = a_conv.shape[0]
    hb = tm // CONV_HALO
    fixed = lambda i: (0, 0)
    return pl.pallas_call(
        functools.partial(_conv_kernel, tm=tm, rc=32),
        grid=(s // tm,),
        in_specs=[pl.BlockSpec((tm, 2 * CONV_CH), lambda i: (i, 0)),
                  pl.BlockSpec((CONV_HALO, 2 * CONV_CH), lambda i: (jnp.maximum(i * hb - 1, 0), 0)),
                  pl.BlockSpec((CONV_WIDTH, CONV_CH), fixed),
                  pl.BlockSpec((1, CONV_CH), fixed), pl.BlockSpec((1, CONV_CH), fixed),
                  pl.BlockSpec((1, CONV_CH), fixed)],
        out_specs=pl.BlockSpec((tm, CONV_CH), lambda i: (i, 0)),
        out_shape=jax.ShapeDtypeStruct((s, CONV_CH), BF16),
        scratch_shapes=[pltpu.VMEM((tm + CONV_HALO, CONV_CH), F32)],
        compiler_params=_cparams(1),
        name="conv_module",
    )(a_conv, a_conv, dw_w, dw_b, ln_g, ln_b)


def _mla_proj_kernel(ql_ref, kvl_ref, kr_ref, cos_ref, sin_ref, gqa_ref, gkva_ref, wuq_ref, wukv_ref,
                     gq_ref, gk_ref, q_out, k_out, vt_out):
    def rms(v, g):
        return v * lax.rsqrt(jnp.mean(v * v, axis=-1, keepdims=True) + EPS) * g

    q = jnp.dot(rms(ql_ref[...], gqa_ref[...]).astype(BF16), wuq_ref[...], preferred_element_type=F32)
    kv = jnp.dot(rms(kvl_ref[...], gkva_ref[...]).astype(BF16), wukv_ref[...], preferred_element_type=F32)
    kr = kr_ref[...]
    cos = cos_ref[...]
    sin = sin_ref[...]
    lane = lax.broadcasted_iota(jnp.int32, kr.shape, 1)
    half = QK_ROPE // 2

    def rope(v):
        rot = jnp.where(lane < half, pltpu.roll(v, LANES - half, 1), pltpu.roll(v, half, 1))
        return v * cos + rot * sin

    gq = gq_ref[...]
    gk = gk_ref[...]
    scale = QK_HEAD ** -0.5 * LOG2_E
    ss_kr = jnp.sum(kr * kr, axis=-1, keepdims=True)
    for h in range(MLA_HEADS):
        qh = q[:, h * HEAD_PAD:(h + 1) * HEAD_PAD]
        rq = lax.rsqrt(jnp.sum(qh * qh, axis=-1, keepdims=True) / QK_HEAD + EPS)
        qh = qh * rq * gq
        q_out[h, :, 0:QK_NOPE] = (qh[:, :QK_NOPE] * scale).astype(BF16)
        q_out[h, :, QK_NOPE:HEAD_PAD] = (rope(qh[:, QK_NOPE:]) * scale).astype(BF16)
        kn = kv[:, h * HEAD_PAD:h * HEAD_PAD + QK_NOPE]
        vh = kv[:, h * HEAD_PAD + QK_NOPE:(h + 1) * HEAD_PAD]
        rk = lax.rsqrt((jnp.sum(kn * kn, axis=-1, keepdims=True) + ss_kr) / QK_HEAD + EPS)
        k_out[h, :, 0:QK_NOPE] = (kn * rk * gk[:, :QK_NOPE]).astype(BF16)
        k_out[h, :, QK_NOPE:HEAD_PAD] = rope(kr * rk * gk[:, QK_NOPE:]).astype(BF16)
        vt_out[h] = vh.T.astype(BF16)


def _mla_proj(q_lat, kv_lat, k_rope, cos_t, sin_t, gqa, gkva, wuq_pad, wukv, gq_pad, gk_pad, tm):
    s = q_lat.shape[0]
    row = lambda i: (i, 0)
    fixed = lambda i: (0, 0)
    hq = MLA_HEADS * HEAD_PAD
    return pl.pallas_call(
        _mla_proj_kernel,
        grid=(s // tm,),
        in_specs=[pl.BlockSpec((tm, Q_LORA), row), pl.BlockSpec((tm, KV_LORA), row),
                  pl.BlockSpec((tm, LANES), row), pl.BlockSpec((tm, LANES), row), pl.BlockSpec((tm, LANES), row),
                  pl.BlockSpec((1, Q_LORA), fixed), pl.BlockSpec((1, KV_LORA), fixed),
                  pl.BlockSpec((Q_LORA, hq), fixed), pl.BlockSpec((KV_LORA, hq), fixed),
                  pl.BlockSpec((1, HEAD_PAD), fixed), pl.BlockSpec((1, HEAD_PAD), fixed)],
        out_specs=[pl.BlockSpec((MLA_HEADS, tm, HEAD_PAD), lambda i: (0, i, 0)),
                   pl.BlockSpec((MLA_HEADS, tm, HEAD_PAD), lambda i: (0, i, 0)),
                   pl.BlockSpec((MLA_HEADS, V_HEAD, tm), lambda i: (0, 0, i))],
        out_shape=[jax.ShapeDtypeStruct((MLA_HEADS, s, HEAD_PAD), BF16),
                   jax.ShapeDtypeStruct((MLA_HEADS, s, HEAD_PAD), BF16),
                   jax.ShapeDtypeStruct((MLA_HEADS, V_HEAD, s), BF16)],
        compiler_params=_cparams(1),
        name="mla_proj",
    )(q_lat, kv_lat, k_rope, cos_t, sin_t, gqa, gkva, wuq_pad, wukv, gq_pad, gk_pad)


def _attn_kernel(q_ref, k_ref, vt_ref, o_ref, *, tq, heads):
    qi = pl.program_id(1)

    def step(kt, carry, masked):
        k0 = pl.multiple_of(kt * tq, tq)
        out = []
        for hh in range(heads):
            m, l, acc = carry[hh]
            s = _nt_dot(k_ref[hh, pl.ds(k0, tq), :], q_ref[hh])
            if masked:
                kc = (k0 + lax.broadcasted_iota(jnp.int32, s.shape, 0)) // CHUNK
                qc = (qi * tq + lax.broadcasted_iota(jnp.int32, s.shape, 1)) // CHUNK
                s = jnp.where(kc <= qc, s, NEG_INF)
            m_new = jnp.maximum(m, jnp.max(s, axis=0, keepdims=True))
            alpha = jnp.exp2(m - m_new)
            p = jnp.exp2(s - m_new)
            l = alpha * l + jnp.sum(p, axis=0, keepdims=True)
            pv = jnp.dot(vt_ref[hh, :, pl.ds(k0, tq)], p.astype(BF16), preferred_element_type=F32)
            out.append((m_new, l, alpha * acc + pv))
        return tuple(out)

    one = (jnp.full((1, tq), NEG_INF, F32), jnp.zeros((1, tq), F32), jnp.zeros((V_HEAD, tq), F32))
    carry = lax.fori_loop(0, qi, lambda kt, c: step(kt, c, False), (one,) * heads)
    carry = step(qi, carry, True)
    for hh in range(heads):
        _, l, acc = carry[hh]
        o_ref[:, hh * V_HEAD:(hh + 1) * V_HEAD] = (acc / l).T.astype(BF16)


def _attention(q, k, vt, tq, heads=2):
    h, s, _ = q.shape
    return pl.pallas_call(
        functools.partial(_attn_kernel, tq=tq, heads=heads),
        grid=(h // heads, s // tq),
        in_specs=[pl.BlockSpec((heads, tq, HEAD_PAD), lambda g, i: (g, i, 0)),
                  pl.BlockSpec((heads, s, HEAD_PAD), lambda g, i: (g, 0, 0)),
                  pl.BlockSpec((heads, V_HEAD, s), lambda g, i: (g, 0, 0))],
        out_specs=pl.BlockSpec((tq, heads * V_HEAD), lambda g, i: (i, g)),
        out_shape=jax.ShapeDtypeStruct((s, h * V_HEAD), BF16),
        compiler_params=_cparams(2),
        name="attention",
    )(q, k, vt)


def _outproj_kernel(x_ref, conv_ref, attn_ref, w_ref, gate_ref, g_ref, sc_ref, sh_ref, x1_ref, h2_ref,
                    h2t_ref):
    mix = jnp.dot(conv_ref[...], w_ref[0:CONV_CH, :], preferred_element_type=F32)
    mix = mix + jnp.dot(attn_ref[...], w_ref[CONV_CH:, :], preferred_element_type=F32)
    x1 = x_ref[...] + gate_ref[...] * mix
    x1_ref[...] = x1
    ms = jnp.mean(x1 * x1, axis=-1, keepdims=True)
    h2 = x1 * lax.rsqrt(ms + EPS) * g_ref[...]
    h2 = h2 * (1.0 + sc_ref[...]) + sh_ref[...]
    h2_ref[...] = h2.astype(BF16)
    h2t_ref[...] = h2.T.astype(BF16)


def _out_proj(x, conv_out, attn_out, w_out, gate1, g2, scale2, shift2, tm):
    s, d = x.shape
    kdim = w_out.shape[0]
    row = lambda i: (i, 0)
    fixed = lambda i: (0, 0)
    return pl.pallas_call(
        _outproj_kernel,
        grid=(s // tm,),
        in_specs=[pl.BlockSpec((tm, d), row), pl.BlockSpec((tm, CONV_CH), row),
                  pl.BlockSpec((tm, kdim - CONV_CH), row),
                  pl.BlockSpec((kdim, d), fixed, pipeline_mode=pl.Buffered(1)),
                  pl.BlockSpec((1, d), fixed), pl.BlockSpec((1, d), fixed),
                  pl.BlockSpec((1, d), fixed), pl.BlockSpec((1, d), fixed)],
        out_specs=[pl.BlockSpec((tm, d), row), pl.BlockSpec((tm, d), row),
                   pl.BlockSpec((d, tm), lambda i: (0, i))],
        out_shape=[jax.ShapeDtypeStruct((s, d), F32), jax.ShapeDtypeStruct((s, d), BF16),
                   jax.ShapeDtypeStruct((d, s), BF16)],
        compiler_params=_cparams(1),
        name="out_proj",
    )(x, conv_out, attn_out, w_out, gate1, g2, scale2, shift2)


def _peer_scores_kernel(h2_ref, wq_ref, k1_ref, k2_ref, s1_ref, s2_ref):
    q = jnp.dot(h2_ref[...], wq_ref[...], preferred_element_type=F32).astype(BF16)
    for h in range(PEER_HEADS):
        base = h * 2 * PEER_HALF
        s1_ref[h] = _nt_dot(k1_ref[h], q[:, base:base + PEER_HALF])
        s2_ref[h] = _nt_dot(k2_ref[h], q[:, base + PEER_HALF:base + 2 * PEER_HALF])


def _peer_scores(h2, wq, k1, k2, tm):
    t, d = h2.shape
    spec = pl.BlockSpec((PEER_HEADS, N_KEYS, tm), lambda i: (0, 0, i))
    keys = pl.BlockSpec((PEER_HEADS, N_KEYS, PEER_HALF), lambda i: (0, 0, 0))
    shape = jax.ShapeDtypeStruct((PEER_HEADS, N_KEYS, t), F32)
    return pl.pallas_call(
        _peer_scores_kernel,
        grid=(t // tm,),
        in_specs=[pl.BlockSpec((tm, d), lambda i: (i, 0)),
                  pl.BlockSpec(wq.shape, lambda i: (0, 0), pipeline_mode=pl.Buffered(1)),
                  keys, keys],
        out_specs=[spec, spec],
        out_shape=[shape, shape],
        compiler_params=_cparams(1),
        name="peer_scores",
    )(h2, wq, k1, k2)


_CAND_ROWS = 80


def _candidate_table():
    tbl = np.full((_CAND_ROWS,), -1, np.int32)
    for i in range(PEER_TOPK):
        tbl[i] = i * PEER_TOPK
    for j in range(1, 8):
        for i in range(8):
            if (i + 1) * (j + 1) <= PEER_TOPK:
                tbl[16 + 8 * (j - 1) + i] = i * PEER_TOPK + j
    for j in range(8, PEER_TOPK):
        tbl[72 + j - 8] = j
    return np.broadcast_to(tbl[:, None], (_CAND_ROWS, LANES)).copy()


def _top16_rows(s):
    row = lax.broadcasted_iota(jnp.int32, s.shape, 0)
    rank = jnp.full(s.shape, float(PEER_TOPK), F32)
    vals = []
    for r in range(PEER_TOPK):
        m = jnp.max(s, axis=0, keepdims=True)
        first = jnp.min(jnp.where(s == m, row, N_KEYS), axis=0, keepdims=True)
        sel = row == first
        rank = jnp.where(sel, float(r), rank)
        s = jnp.where(sel, NEG_INF, s)
        vals.append(m)
    return jnp.concatenate(vals, axis=0), rank


def _peer_select_kernel(s1_ref, s2_ref, tbl_ref, n1_ref, a1_ref, r2_ref, a2_ref, *, chunks):
    flat = tbl_ref[...]
    valid = flat >= 0
    row8 = lax.broadcasted_iota(jnp.int32, (8, LANES), 0)
    def chunk(c, _):
        sl = pl.ds(pl.multiple_of(c * LANES, LANES), LANES)
        s1 = s1_ref[0, :, sl]
        s2 = s2_ref[0, :, sl]
        v1, rank1 = _top16_rows(s1)
        v2, rank2 = _top16_rows(s2)
        blocks = [v1 + v2[0:1]]
        for j in range(1, 8):
            blocks.append(v1[0:8] + v2[j:j + 1])
        blocks.append(v1[0:1] + v2[8:16])
        cand0 = jnp.where(valid, jnp.concatenate(blocks, axis=0), NEG_INF)
        cand = cand0
        chosen = jnp.zeros(cand.shape, F32)
        for _ in range(PEER_TOPK):
            m = jnp.max(cand, axis=0, keepdims=True)
            first = jnp.min(jnp.where(cand == m, flat, PEER_TOPK * PEER_TOPK), axis=0, keepdims=True)
            sel = flat == first
            chosen = jnp.where(sel, 1.0, chosen)
            cand = jnp.where(sel, NEG_INF, cand)
        n_lo = chosen[0:8]
        for j in range(1, 8):
            n_lo = n_lo + chosen[16 + 8 * (j - 1):16 + 8 * j]
        tail = jnp.sum(chosen[72:80], axis=0, keepdims=True)
        n_lo = n_lo + jnp.where(row8 == 0, tail, 0.0)
        n = jnp.concatenate([n_lo, chosen[8:16]], axis=0)
        top = v1[0:1] + v2[0:1]
        z = jnp.sum(jnp.where(chosen > 0.0, jnp.exp(cand0 - top), 0.0), axis=0, keepdims=True)
        n1 = jnp.zeros(s1.shape, F32)
        for r in range(PEER_TOPK):
            n1 = jnp.where(rank1 == float(r), n[r:r + 1], n1)
        n1_ref[0, :, sl] = n1 * RANK_SCALE
        a1_ref[0, :, sl] = jnp.exp(s1 - v1[0:1])
        r2_ref[0, :, sl] = (rank2 * RANK_SCALE).astype(BF16)
        a2_ref[0, :, sl] = (jnp.exp(s2 - v2[0:1]) / z).astype(BF16)
        return 0

    lax.fori_loop(0, chunks, chunk, 0)


def _peer_select(s1, s2, tl):
    h, n, t = s1.shape
    spec = pl.BlockSpec((1, n, tl), lambda hh, i: (hh, 0, i))
    wide = jax.ShapeDtypeStruct((h, n, t), F32)
    narrow = jax.ShapeDtypeStruct((h, n, t), BF16)
    tbl = jnp.asarray(_candidate_table())
    return pl.pallas_call(
        functools.partial(_peer_select_kernel, chunks=tl // LANES),
        grid=(h, t // tl),
        in_specs=[spec, spec, pl.BlockSpec((_CAND_ROWS, LANES), lambda hh, i: (0, 0))],
        out_specs=[spec] * 4,
        out_shape=[wide, wide, narrow, narrow],
        compiler_params=_cparams(2),
        name="peer_select",
    )(s1, s2, tbl)


def _gelu_tanh(x):
    return 0.5 * x * (1.0 + jnp.tanh(0.7978845608028654 * (x + 0.044715 * (x * x * x))))


def _zero_of(x):
    bits = lax.bitcast_convert_type(x, jnp.uint32)
    bits = lax.shift_right_logical(lax.shift_right_logical(bits, jnp.uint32(16)), jnp.uint32(16))
    return lax.bitcast_convert_type(bits, F32)


BF16_ROWS = 16


def _rows_bf16(row, n):
    tile = jnp.broadcast_to(row, (BF16_ROWS, row.shape[1])).astype(BF16)
    return jnp.tile(tile, (n // BF16_ROWS, 1))


def _peer_dense_kernel(h2t_ref, u_ref, v_ref, n1_ref, a1_ref, r2_ref, a2_ref, x1_ref, gate_ref, o_ref,
                       *a_scr, te, chunk):
    j = pl.program_id(1)

    @pl.when(j == 0)
    def _():
        o_ref[...] = jnp.zeros_like(o_ref)

    h2t = h2t_ref[...]
    zero = jnp.zeros((), BF16)
    per = chunk // N_KEYS
    n_chunks = te // chunk
    slot_w = lax.rem(j, 2)
    slot_r = lax.rem(j + 2, 2)

    def scores(q):
        a_scr[q][slot_w] = jnp.dot(u_ref[pl.ds(q * chunk, chunk), :], h2t, preferred_element_type=F32)

    scores(0)
    for q in range(n_chunks):
        if q + 1 < n_chunks:
            scores(q + 1)
        blocks = []
        for e in range(per):
            e1 = q * per + e
            pieces = []
            for c in range(h2t.shape[1] // LANES):
                cols = pl.ds(c * LANES, LANES)
                terms = []
                for h in range(PEER_HEADS):
                    n1 = _rows_bf16(n1_ref[h, pl.ds(e1, 1), cols], N_KEYS)
                    a1 = _rows_bf16(a1_ref[h, pl.ds(e1, 1), cols], N_KEYS)
                    keys = pl.ds(h * N_KEYS, N_KEYS)
                    chosen = jnp.maximum(n1 - r2_ref[keys, cols], zero)
                    terms.append(jnp.minimum(a1 * a2_ref[keys, cols], chosen))
                while len(terms) > 1:
                    terms = [terms[k] + terms[k + 1] for k in range(0, len(terms), 2)]
                a_t = a_scr[q][slot_r, pl.ds(e * N_KEYS, N_KEYS), cols]
                pieces.append(terms[0] * _gelu_tanh(a_t).astype(BF16))
            blocks.append(jnp.concatenate(pieces, axis=1))
        act = jnp.concatenate(blocks, axis=0)
        if q + 1 < n_chunks:
            act = act + _zero_of(a_scr[q + 1][slot_r, 0:1, :]).astype(BF16)
        o_ref[...] += _tn_dot(act, v_ref[pl.ds(q * chunk, chunk), :])

    @pl.when(j == pl.num_programs(1) - 1)
    def _():
        o_ref[...] = x1_ref[...] + gate_ref[...] * o_ref[...]


def _peer_dense(h2t, u, v, n1, a1, r2, a2, x1, gate2, tm, te):
    d, t = h2t.shape
    n_j = u.shape[0] // te
    e1_per_tile = te // N_KEYS
    single = pl.Buffered(1)
    small = pl.BlockSpec((PEER_HEADS, e1_per_tile, tm), lambda i, j: (0, j, i))
    big = pl.BlockSpec((PEER_HEADS * N_KEYS, tm), lambda i, j: (0, i), pipeline_mode=single)
    r2 = r2.reshape(PEER_HEADS * N_KEYS, t)
    a2 = a2.reshape(PEER_HEADS * N_KEYS, t)
    chunk = 2 * N_KEYS
    return pl.pallas_call(
        functools.partial(_peer_dense_kernel, te=te, chunk=chunk),
        grid=(t // tm, n_j),
        in_specs=[pl.BlockSpec((d, tm), lambda i, j: (0, i), pipeline_mode=single),
                  pl.BlockSpec((te, d), lambda i, j: (j, 0)),
                  pl.BlockSpec((te, d), lambda i, j: (j, 0)),
                  small, small, big, big,
                  pl.BlockSpec((tm, d), lambda i, j: (i, 0), pipeline_mode=single),
                  pl.BlockSpec((1, d), lambda i, j: (0, 0))],
        out_specs=pl.BlockSpec((tm, d), lambda i, j: (i, 0)),
        out_shape=jax.ShapeDtypeStruct((t, d), F32),
        scratch_shapes=[pltpu.VMEM((2, chunk, tm), F32)] * (te // chunk),
        compiler_params=_cparams(2),
        name="peer_dense",
    )(h2t, u, v, n1, a1, r2, a2, x1, gate2)


def _pad_cols(w, width):
    return jnp.pad(w, ((0, 0), (0, width - w.shape[1])))


def _head_pad_cols(w, per_head):
    k = w.shape[0]
    w = w.reshape(k, MLA_HEADS, per_head)
    w = jnp.pad(w, ((0, 0), (0, 0), (0, HEAD_PAD - per_head)))
    return w.reshape(k, MLA_HEADS * HEAD_PAD)


def _layer(x, mod, positions_tables, p, tiles):
    d = x.shape[1]
    shift1, scale1, gate1, shift2, scale2, gate2 = [mod[:, i * d:(i + 1) * d] for i in range(6)]
    cos_t, sin_t = positions_tables

    w_in = _pad_cols(p["w_in"], 2 * CONV_CH + Q_LORA + KV_LORA + LANES).astype(BF16)
    a_conv, q_lat, kv_lat, k_rope = _in_proj(x, p["norm1_g"], scale1, shift1, w_in, tiles["proj"])

    conv_out = _conv_module(a_conv, p["conv_dw_w"], p["conv_dw_b"], p["conv_ln_g"], p["conv_ln_b"],
                            tiles["conv"])

    wuq = _head_pad_cols(p["w_uq"], QK_HEAD).astype(BF16)
    gq = _pad_cols(p["q_norm_g"], HEAD_PAD)
    gk = _pad_cols(p["k_norm_g"], HEAD_PAD)
    q, k, vt = _mla_proj(q_lat, kv_lat, k_rope, cos_t, sin_t, p["q_a_norm_g"], p["kv_a_norm_g"],
                         wuq, p["w_ukv"].astype(BF16), gq, gk, tiles["proj"])
    attn_out = _attention(q, k, vt, tiles["attn"])

    x1, h2, h2t = _out_proj(x, conv_out, attn_out, p["w_out"].astype(BF16), gate1, p["norm2_g"], scale2,
                            shift2, tiles["proj"])

    s1, s2 = _peer_scores(h2, p["peer_wq"].astype(BF16), p["peer_k1"].astype(BF16),
                          p["peer_k2"].astype(BF16), tiles["proj"])
    n1, a1, r2, a2 = _peer_select(s1, s2, tiles["select"])
    return _peer_dense(h2t, p["peer_u"].astype(BF16), p["peer_v"].astype(BF16), n1, a1, r2, a2, x1, gate2,
                       tiles["dense_tokens"], tiles["dense_experts"])


def _tiles(s):
    return {"proj": min(512, s), "conv": min(256, s), "attn": min(512, s), "select": min(1024, s),
            "dense_tokens": min(512, s), "dense_experts": 1024}


def kernel(x, c, positions, w_ada, b_ada, norm1_g, w_in, conv_dw_w, conv_dw_b, conv_ln_g, conv_ln_b,
           q_a_norm_g, w_uq, kv_a_norm_g, w_ukv, q_norm_g, k_norm_g, w_out, norm2_g, peer_wq, peer_k1,
           peer_k2, peer_u, peer_v):
    b, s, d = x.shape
    assert b == 1, "kernel is written for batch 1"
    depth = w_ada.shape[0]
    row = lambda a, l: a[l].reshape(1, -1)
    tables = _rope_tables(positions[0])
    xs = x[0]
    for l in range(depth):
        mod = _ada_mod(c, w_ada[l], b_ada[l])
        params = {
            "norm1_g": row(norm1_g, l), "w_in": w_in[l], "conv_dw_w": conv_dw_w[l],
            "conv_dw_b": row(conv_dw_b, l), "conv_ln_g": row(conv_ln_g, l), "conv_ln_b": row(conv_ln_b, l),
            "q_a_norm_g": row(q_a_norm_g, l), "w_uq": w_uq[l], "kv_a_norm_g": row(kv_a_norm_g, l),
            "w_ukv": w_ukv[l], "q_norm_g": row(q_norm_g, l), "k_norm_g": row(k_norm_g, l),
            "w_out": w_out[l], "norm2_g": row(norm2_g, l), "peer_wq": peer_wq[l],
            "peer_k1": peer_k1[l], "peer_k2": peer_k2[l], "peer_u": peer_u[l], "peer_v": peer_v[l],
        }
        xs = _layer(xs, mod, tables, params, _tiles(s))
    return xs[None]
```

```python
import functools

import numpy as np
import jax
import jax.numpy as jnp
from jax import lax
from jax.experimental import pallas as pl
from jax.experimental.pallas import tpu as pltpu

F32 = jnp.float32
BF16 = jnp.bfloat16

CHUNK = 64
EPS = 1e-6
CONV_CH = 1024
CONV_WIDTH = 31
MLA_HEADS = 8
QK_NOPE = 128
QK_ROPE = 64
QK_HEAD = QK_NOPE + QK_ROPE
V_HEAD = 128
Q_LORA = 768
KV_LORA = 512
ROPE_THETA = 10000.0
PEER_HEADS = 8
N_KEYS = 128
PEER_TOPK = 16
PEER_HALF = 128

LANES = 128
SUBLANES = 8
HEAD_PAD = 256
CONV_HALO = 32
VMEM_LIMIT = 56 * 1024 * 1024

RANK_SCALE = 1024.0
NEG_INF = float("-inf")
LOG2_E = 1.4426950408889634


def _cparams(n_axes, vmem=VMEM_LIMIT, flags=None):
    return pltpu.CompilerParams(dimension_semantics=("arbitrary",) * n_axes, vmem_limit_bytes=vmem,
                                flags=flags)


def _nt_dot(a, b):
    return lax.dot_general(a, b, (((1,), (1,)), ((), ())), preferred_element_type=F32)


def _tn_dot(a, b):
    return lax.dot_general(a, b, (((0,), (0,)), ((), ())), preferred_element_type=F32)


def _ada_kernel(c_ref, w_ref, b_ref, o_ref):
    c = c_ref[...]
    ca = (c * jax.nn.sigmoid(c)).astype(BF16)
    o_ref[...] = jnp.dot(ca, w_ref[...].astype(BF16), preferred_element_type=F32) + b_ref[...]


def _ada_mod(c, w_ada, b_ada):
    d = c.shape[1]
    n = w_ada.shape[1]
    tn = 1024
    c8 = jnp.broadcast_to(c, (8, d))
    out = pl.pallas_call(
        _ada_kernel,
        grid=(n // tn,),
        in_specs=[pl.BlockSpec((8, d), lambda j: (0, 0)),
                  pl.BlockSpec((d, tn), lambda j: (0, j)),
                  pl.BlockSpec((1, tn), lambda j: (0, j))],
        out_specs=pl.BlockSpec((8, tn), lambda j: (0, j)),
        out_shape=jax.ShapeDtypeStruct((8, n), F32),
        compiler_params=_cparams(1),
        name="ada_mod",
    )(c8, w_ada, b_ada.reshape(1, n))
    return out[0:1]


def _rope_table_kernel(pos_ref, invf_ref, cos_ref, sin_ref):
    ang = pos_ref[...] * invf_ref[...]
    cos_ref[...] = jnp.cos(ang)
    sin_ref[...] = jnp.sin(ang)


def _rope_tables(positions):
    s = positions.shape[0]
    half = QK_ROPE // 2
    rep = LANES // half
    inv_freq = ROPE_THETA ** (-jnp.arange(0, QK_ROPE, 2, dtype=F32) / QK_ROPE)
    pos_rep = jnp.repeat(positions.astype(F32), half).reshape(s // rep, LANES)
    invf = jnp.tile(inv_freq, rep).reshape(1, LANES)
    cos_d, sin_d = pl.pallas_call(
        _rope_table_kernel,
        out_shape=(jax.ShapeDtypeStruct((s // rep, LANES), F32),) * 2,
        name="rope_table",
    )(pos_rep, invf)
    cos = cos_d.reshape(s, half)
    sin = sin_d.reshape(s, half)
    zeros = jnp.zeros((s, LANES - QK_ROPE), F32)
    return (jnp.concatenate([cos, cos, zeros], axis=1),
            jnp.concatenate([-sin, sin, zeros], axis=1))


def _inproj_kernel(x_ref, g_ref, sc_ref, sh_ref, w_ref, conv_ref, ql_ref, kvl_ref, kr_ref):
    x = x_ref[...]
    ms = jnp.mean(x * x, axis=-1, keepdims=True)
    h = x * lax.rsqrt(ms + EPS) * g_ref[...]
    h = h * (1.0 + sc_ref[...]) + sh_ref[...]
    a = jnp.dot(h.astype(BF16), w_ref[...], preferred_element_type=F32)
    o1 = 2 * CONV_CH
    o2 = o1 + Q_LORA
    o3 = o2 + KV_LORA
    conv_ref[...] = a[:, :o1]
    ql_ref[...] = a[:, o1:o2]
    kvl_ref[...] = a[:, o2:o3]
    kr_ref[...] = a[:, o3:o3 + LANES]


def _in_proj(x, g, scale, shift, w_in_pad, tm):
    s, d = x.shape
    n = w_in_pad.shape[1]
    row = lambda i: (i, 0)
    fixed = lambda i: (0, 0)
    return pl.pallas_call(
        _inproj_kernel,
        grid=(s // tm,),
        in_specs=[pl.BlockSpec((tm, d), row),
                  pl.BlockSpec((1, d), fixed), pl.BlockSpec((1, d), fixed), pl.BlockSpec((1, d), fixed),
                  pl.BlockSpec((d, n), fixed, pipeline_mode=pl.Buffered(1))],
        out_specs=[pl.BlockSpec((tm, 2 * CONV_CH), row), pl.BlockSpec((tm, Q_LORA), row),
                   pl.BlockSpec((tm, KV_LORA), row), pl.BlockSpec((tm, LANES), row)],
        out_shape=[jax.ShapeDtypeStruct((s, 2 * CONV_CH), F32), jax.ShapeDtypeStruct((s, Q_LORA), F32),
                   jax.ShapeDtypeStruct((s, KV_LORA), F32), jax.ShapeDtypeStruct((s, LANES), F32)],
        compiler_params=_cparams(1),
        name="in_proj",
    )(x, g, scale, shift, w_in_pad)


def _conv_kernel(cur_ref, halo_ref, w_ref, b_ref, g_ref, beta_ref, o_ref, hbuf, *, tm, rc):
    i = pl.program_id(0)
    cur = cur_ref[...]
    hbuf[pl.ds(CONV_HALO, tm), :] = cur[:, :CONV_CH] * jax.nn.sigmoid(cur[:, CONV_CH:])
    hal = halo_ref[...]
    hg = hal[:, :CONV_CH] * jax.nn.sigmoid(hal[:, CONV_CH:])
    hbuf[pl.ds(0, CONV_HALO), :] = jnp.where(i > 0, hg, 0.0)
    first = CONV_HALO - (CONV_WIDTH - 1)
    span = rc + CONV_HALO

    def row_chunk(r, _):
        r0 = pl.multiple_of(r * rc, rc)
        accs = []
        for c in range(CONV_CH // LANES):
            cols = pl.ds(c * LANES, LANES)
            x = hbuf[pl.ds(r0, span), cols]
            acc = jnp.zeros((rc, LANES), F32)
            for res in range(SUBLANES):
                taps = [k for k in range(CONV_WIDTH) if (first + k) % SUBLANES == res]
                xs = x if res == 0 else pltpu.roll(x, span - res, 0)
                for k in taps:
                    a = (first + k) // SUBLANES
                    acc = acc + w_ref[pl.ds(k, 1), cols] * xs[SUBLANES * a:SUBLANES * a + rc]
            accs.append(acc)
        acc = jnp.concatenate(accs, axis=1) + b_ref[...]
        mu = jnp.mean(acc, axis=-1, keepdims=True)
        cen = acc - mu
        var = jnp.mean(cen * cen, axis=-1, keepdims=True)
        y = cen * lax.rsqrt(var + EPS) * g_ref[...] + beta_ref[...]
        o_ref[pl.ds(r0, rc), :] = (y * jax.nn.sigmoid(y)).astype(BF16)
        return 0

    lax.fori_loop(0, tm // rc, row_chunk, 0)


def _conv_module(a_conv, dw_w, dw_b, ln_g, ln_b, tm):
    s = a_conv.shape[0]
    hb = tm // CONV_HALO
    fixed = lambda i: (0, 0)
    return pl.pallas_call(
        functools.partial(_conv_kernel, tm=tm, rc=32),
        grid=(s // tm,),
        in_specs=[pl.BlockSpec((tm, 2 * CONV_CH), lambda i: (i, 0)),
                  pl.BlockSpec((CONV_HALO, 2 * CONV_CH), lambda i: (jnp.maximum(i * hb - 1, 0), 0)),
                  pl.BlockSpec((CONV_WIDTH, CONV_CH), fixed),
                  pl.BlockSpec((1, CONV_CH), fixed), pl.BlockSpec((1, CONV_CH), fixed),
                  pl.BlockSpec((1, CONV_CH), fixed)],
        out_specs=pl.BlockSpec((tm, CONV_CH), lambda i: (i, 0)),
        out_shape=jax.ShapeDtypeStruct((s, CONV_CH), BF16),
        scratch_shapes=[pltpu.VMEM((tm + CONV_HALO, CONV_CH), F32)],
        compiler_params=_cparams(1),
        name="conv_module",
    )(a_conv, a_conv, dw_w, dw_b, ln_g, ln_b)


def _mla_proj_kernel(ql_ref, kvl_ref, kr_ref, cos_ref, sin_ref, gqa_ref, gkva_ref, wuq_ref, wukv_ref,
                     gq_ref, gk_ref, q_out, k_out, vt_out):
    def rms(v, g):
        return v * lax.rsqrt(jnp.mean(v * v, axis=-1, keepdims=True) + EPS) * g

    q = jnp.dot(rms(ql_ref[...], gqa_ref[...]).astype(BF16), wuq_ref[...], preferred_element_type=F32)
    kv = jnp.dot(rms(kvl_ref[...], gkva_ref[...]).astype(BF16), wukv_ref[...], preferred_element_type=F32)
    kr = kr_ref[...]
    cos = cos_ref[...]
    sin = sin_ref[...]
    lane = lax.broadcasted_iota(jnp.int32, kr.shape, 1)
    half = QK_ROPE // 2

    def rope(v):
        rot = jnp.where(lane < half, pltpu.roll(v, LANES - half, 1), pltpu.roll(v, half, 1))
        return v * cos + rot * sin

    gq = gq_ref[...]
    gk = gk_ref[...]
    scale = QK_HEAD ** -0.5 * LOG2_E
    ss_kr = jnp.sum(kr * kr, axis=-1, keepdims=True)
    for h in range(MLA_HEADS):
        qh = q[:, h * HEAD_PAD:(h + 1) * HEAD_PAD]
        rq = lax.rsqrt(jnp.sum(qh * qh, axis=-1, keepdims=True) / QK_HEAD + EPS)
        qh = qh * rq * gq
        q_out[h, :, 0:QK_NOPE] = (qh[:, :QK_NOPE] * scale).astype(BF16)
        q_out[h, :, QK_NOPE:HEAD_PAD] = (rope(qh[:, QK_NOPE:]) * scale).astype(BF16)
        kn = kv[:, h * HEAD_PAD:h * HEAD_PAD + QK_NOPE]
        vh = kv[:, h * HEAD_PAD + QK_NOPE:(h + 1) * HEAD_PAD]
        rk = lax.rsqrt((jnp.sum(kn * kn, axis=-1, keepdims=True) + ss_kr) / QK_HEAD + EPS)
        k_out[h, :, 0:QK_NOPE] = (kn * rk * gk[:, :QK_NOPE]).astype(BF16)
        k_out[h, :, QK_NOPE:HEAD_PAD] = rope(kr * rk * gk[:, QK_NOPE:]).astype(BF16)
        vt_out[h] = vh.T.astype(BF16)


def _mla_proj(q_lat, kv_lat, k_rope, cos_t, sin_t, gqa, gkva, wuq_pad, wukv, gq_pad, gk_pad, tm):
    s = q_lat.shape[0]
    row = lambda i: (i, 0)
    fixed = lambda i: (0, 0)
    hq = MLA_HEADS * HEAD_PAD
    return pl.pallas_call(
        _mla_proj_kernel,
        grid=(s // tm,),
        in_specs=[pl.BlockSpec((tm, Q_LORA), row), pl.BlockSpec((tm, KV_LORA), row),
                  pl.BlockSpec((tm, LANES), row), pl.BlockSpec((tm, LANES), row), pl.BlockSpec((tm, LANES), row),
                  pl.BlockSpec((1, Q_LORA), fixed), pl.BlockSpec((1, KV_LORA), fixed),
                  pl.BlockSpec((Q_LORA, hq), fixed), pl.BlockSpec((KV_LORA, hq), fixed),
                  pl.BlockSpec((1, HEAD_PAD), fixed), pl.BlockSpec((1, HEAD_PAD), fixed)],
        out_specs=[pl.BlockSpec((MLA_HEADS, tm, HEAD_PAD), lambda i: (0, i, 0)),
                   pl.BlockSpec((MLA_HEADS, tm, HEAD_PAD), lambda i: (0, i, 0)),
                   pl.BlockSpec((MLA_HEADS, V_HEAD, tm), lambda i: (0, 0, i))],
        out_shape=[jax.ShapeDtypeStruct((MLA_HEADS, s, HEAD_PAD), BF16),
                   jax.ShapeDtypeStruct((MLA_HEADS, s, HEAD_PAD), BF16),
                   jax.ShapeDtypeStruct((MLA_HEADS, V_HEAD, s), BF16)],
        compiler_params=_cparams(1),
        name="mla_proj",
    )(q_lat, kv_lat, k_rope, cos_t, sin_t, gqa, gkva, wuq_pad, wukv, gq_pad, gk_pad)


def _attn_kernel(q_ref, k_ref, vt_ref, o_ref, s_scr, mt_scr, m_scr, l_scr, acc_scr, *, tq, heads):
    qi = pl.program_id(1)

    def produce(kt, slot, masked):
        k0 = pl.multiple_of(kt * tq, tq)
        for hh in range(heads):
            s = _nt_dot(k_ref[hh, pl.ds(k0, tq), :], q_ref[hh])
            if masked:
                kc = (k0 + lax.broadcasted_iota(jnp.int32, s.shape, 0)) // CHUNK
                qc = (qi * tq + lax.broadcasted_iota(jnp.int32, s.shape, 1)) // CHUNK
                s = jnp.where(kc <= qc, s, NEG_INF)
            s_scr[hh, slot] = s
            mt_scr[hh, slot] = jnp.max(s, axis=0, keepdims=True)

    def consume(kt, slot):
        k0 = pl.multiple_of(kt * tq, tq)
        for hh in range(heads):
            m = m_scr[hh]
            m_new = jnp.maximum(m, mt_scr[hh, slot])
            alpha = jnp.exp2(m - m_new)
            p = jnp.exp2(s_scr[hh, slot] - m_new)
            l_scr[hh] = alpha * l_scr[hh] + jnp.sum(p, axis=0, keepdims=True)
            pv = jnp.dot(vt_ref[hh, :, pl.ds(k0, tq)], p.astype(BF16), preferred_element_type=F32)
            acc_scr[hh] = alpha * acc_scr[hh] + pv
            m_scr[hh] = m_new

    m_scr[...] = jnp.full(m_scr.shape, NEG_INF, F32)
    l_scr[...] = jnp.zeros_like(l_scr)
    acc_scr[...] = jnp.zeros_like(acc_scr)

    @pl.when(qi == 0)
    def _():
        produce(0, 0, True)

    @pl.when(qi > 0)
    def _():
        produce(0, 0, False)

    def body(kt, _):
        slot = lax.rem(kt, 2)
        consume(kt, slot)
        produce(kt + 1, 1 - slot, False)
        return 0

    lax.fori_loop(0, jnp.maximum(qi - 1, 0), body, 0)

    @pl.when(qi > 0)
    def _():
        slot = lax.rem(qi - 1, 2)
        consume(qi - 1, slot)
        produce(qi, 1 - slot, True)

    consume(qi, lax.rem(qi, 2))
    for hh in range(heads):
        o_ref[:, hh * V_HEAD:(hh + 1) * V_HEAD] = (acc_scr[hh] / l_scr[hh]).T.astype(BF16)


def _attention(q, k, vt, tq, heads=2):
    h, s, _ = q.shape
    return pl.pallas_call(
        functools.partial(_attn_kernel, tq=tq, heads=heads),
        grid=(h // heads, s // tq),
        in_specs=[pl.BlockSpec((heads, tq, HEAD_PAD), lambda g, i: (g, i, 0)),
                  pl.BlockSpec((heads, s, HEAD_PAD), lambda g, i: (g, 0, 0)),
                  pl.BlockSpec((heads, V_HEAD, s), lambda g, i: (g, 0, 0))],
        out_specs=pl.BlockSpec((tq, heads * V_HEAD), lambda g, i: (i, g)),
        out_shape=jax.ShapeDtypeStruct((s, h * V_HEAD), BF16),
        scratch_shapes=[pltpu.VMEM((heads, 2, tq, tq), F32), pltpu.VMEM((heads, 2, 1, tq), F32),
                        pltpu.VMEM((heads, 1, tq), F32), pltpu.VMEM((heads, 1, tq), F32),
                        pltpu.VMEM((heads, V_HEAD, tq), F32)],
        compiler_params=_cparams(2),
        name="attention",
    )(q, k, vt)


def _outproj_kernel(x_ref, conv_ref, attn_ref, w_ref, gate_ref, g_ref, sc_ref, sh_ref, x1_ref, h2_ref,
                    h2t_ref):
    mix = jnp.dot(conv_ref[...], w_ref[0:CONV_CH, :], preferred_element_type=F32)
    mix = mix + jnp.dot(attn_ref[...], w_ref[CONV_CH:, :], preferred_element_type=F32)
    x1 = x_ref[...] + gate_ref[...] * mix
    x1_ref[...] = x1
    ms = jnp.mean(x1 * x1, axis=-1, keepdims=True)
    h2 = x1 * lax.rsqrt(ms + EPS) * g_ref[...]
    h2 = h2 * (1.0 + sc_ref[...]) + sh_ref[...]
    h2_ref[...] = h2.astype(BF16)
    h2t_ref[...] = h2.T.astype(BF16)


def _out_proj(x, conv_out, attn_out, w_out, gate1, g2, scale2, shift2, tm):
    s, d = x.shape
    kdim = w_out.shape[0]
    row = lambda i: (i, 0)
    fixed = lambda i: (0, 0)
    return pl.pallas_call(
        _outproj_kernel,
        grid=(s // tm,),
        in_specs=[pl.BlockSpec((tm, d), row), pl.BlockSpec((tm, CONV_CH), row),
                  pl.BlockSpec((tm, kdim - CONV_CH), row),
                  pl.BlockSpec((kdim, d), fixed, pipeline_mode=pl.Buffered(1)),
                  pl.BlockSpec((1, d), fixed), pl.BlockSpec((1, d), fixed),
                  pl.BlockSpec((1, d), fixed), pl.BlockSpec((1, d), fixed)],
        out_specs=[pl.BlockSpec((tm, d), row), pl.BlockSpec((tm, d), row),
                   pl.BlockSpec((d, tm), lambda i: (0, i))],
        out_shape=[jax.ShapeDtypeStruct((s, d), F32), jax.ShapeDtypeStruct((s, d), BF16),
                   jax.ShapeDtypeStruct((d, s), BF16)],
        compiler_params=_cparams(1),
        name="out_proj",
    )(x, conv_out, attn_out, w_out, gate1, g2, scale2, shift2)


def _peer_scores_kernel(h2_ref, wq_ref, k1_ref, k2_ref, s1_ref, s2_ref):
    q = jnp.dot(h2_ref[...], wq_ref[...], preferred_element_type=F32).astype(BF16)
    for h in range(PEER_HEADS):
        base = h * 2 * PEER_HALF
        s1_ref[h] = _nt_dot(k1_ref[h], q[:, base:base + PEER_HALF])
        s2_ref[h] = _nt_dot(k2_ref[h], q[:, base + PEER_HALF:base + 2 * PEER_HALF])


def _peer_scores(h2, wq, k1, k2, tm):
    t, d = h2.shape
    spec = pl.BlockSpec((PEER_HEADS, N_KEYS, tm), lambda i: (0, 0, i))
    keys = pl.BlockSpec((PEER_HEADS, N_KEYS, PEER_HALF), lambda i: (0, 0, 0))
    shape = jax.ShapeDtypeStruct((PEER_HEADS, N_KEYS, t), F32)
    return pl.pallas_call(
        _peer_scores_kernel,
        grid=(t // tm,),
        in_specs=[pl.BlockSpec((tm, d), lambda i: (i, 0)),
                  pl.BlockSpec(wq.shape, lambda i: (0, 0), pipeline_mode=pl.Buffered(1)),
                  keys, keys],
        out_specs=[spec, spec],
        out_shape=[shape, shape],
        compiler_params=_cparams(1),
        name="peer_scores",
    )(h2, wq, k1, k2)


_CAND_ROWS = 80
SELECT_LANES = 256


def _candidate_table():
    tbl = np.full((_CAND_ROWS,), -1, np.int32)
    for i in range(PEER_TOPK):
        tbl[i] = i * PEER_TOPK
    for j in range(1, 8):
        for i in range(8):
            if (i + 1) * (j + 1) <= PEER_TOPK:
                tbl[16 + 8 * (j - 1) + i] = i * PEER_TOPK + j
    for j in range(8, PEER_TOPK):
        tbl[72 + j - 8] = j
    return np.broadcast_to(tbl[:, None], (_CAND_ROWS, SELECT_LANES)).copy()


def _top16_rows(s, exact_ties):
    row = lax.broadcasted_iota(jnp.int32, s.shape, 0) if exact_ties else None
    rank = jnp.full(s.shape, float(PEER_TOPK), F32)
    vals = []
    for r in range(PEER_TOPK):
        m = jnp.max(s, axis=0, keepdims=True)
        if exact_ties:
            first = jnp.min(jnp.where(s == m, row, N_KEYS), axis=0, keepdims=True)
            sel = row == first
        else:
            sel = s == m
        rank = jnp.where(sel, float(r), rank)
        s = jnp.where(sel, NEG_INF, s)
        vals.append(m)
    return jnp.concatenate(vals, axis=0), rank


def _count_off(marked):
    cnt = jnp.sum(jnp.where(marked, 1.0, 0.0), axis=0, keepdims=True)
    return jnp.where(cnt != float(PEER_TOPK), 1.0, 0.0)


def _select_chunk(s1, s2, flat, exact_ties):
    valid = flat >= 0
    row8 = lax.broadcasted_iota(jnp.int32, (8, SELECT_LANES), 0)
    v1, rank1 = _top16_rows(s1, exact_ties)
    v2, rank2 = _top16_rows(s2, exact_ties)
    blocks = [v1 + v2[0:1]]
    for j in range(1, 8):
        blocks.append(v1[0:8] + v2[j:j + 1])
    blocks.append(v1[0:1] + v2[8:16])
    cand0 = jnp.where(valid, jnp.concatenate(blocks, axis=0), NEG_INF)
    cand = cand0
    chosen = jnp.zeros(cand.shape, F32)
    for _ in range(PEER_TOPK):
        m = jnp.max(cand, axis=0, keepdims=True)
        if exact_ties:
            first = jnp.min(jnp.where(cand == m, flat, PEER_TOPK * PEER_TOPK), axis=0, keepdims=True)
            sel = flat == first
        else:
            sel = cand == m
        chosen = jnp.where(sel, 1.0, chosen)
        cand = jnp.where(sel, NEG_INF, cand)
    n_lo = chosen[0:8]
    for j in range(1, 8):
        n_lo = n_lo + chosen[16 + 8 * (j - 1):16 + 8 * j]
    tail = jnp.sum(chosen[72:80], axis=0, keepdims=True)
    n_lo = n_lo + jnp.where(row8 == 0, tail, 0.0)
    n = jnp.concatenate([n_lo, chosen[8:16]], axis=0)
    top = v1[0:1] + v2[0:1]
    z = jnp.sum(jnp.where(chosen > 0.0, jnp.exp(cand0 - top), 0.0), axis=0, keepdims=True)
    n1 = jnp.zeros(s1.shape, F32)
    for r in range(PEER_TOPK):
        n1 = jnp.where(rank1 == float(r), n[r:r + 1], n1)
    off = (_count_off(rank1 < float(PEER_TOPK)) + _count_off(rank2 < float(PEER_TOPK))
           + _count_off(chosen > 0.0))
    outs = (n1 * RANK_SCALE, jnp.exp(s1 - v1[0:1]), (rank2 * RANK_SCALE).astype(BF16),
            (jnp.exp(s2 - v2[0:1]) / z).astype(BF16))
    return outs, off


def _peer_select_kernel(s1_ref, s2_ref, tbl_ref, n1_ref, a1_ref, r2_ref, a2_ref, *, chunks):
    flat = tbl_ref[...]
    out_refs = (n1_ref, a1_ref, r2_ref, a2_ref)

    def chunk(c, _):
        sl = pl.ds(pl.multiple_of(c * SELECT_LANES, SELECT_LANES), SELECT_LANES)
        s1 = s1_ref[0, :, sl]
        s2 = s2_ref[0, :, sl]
        outs, off = _select_chunk(s1, s2, flat, exact_ties=False)
        for ref, val in zip(out_refs, outs):
            ref[0, :, sl] = val

        @pl.when(jnp.sum(off) > 0.0)
        def _():
            outs_exact, _ = _select_chunk(s1_ref[0, :, sl], s2_ref[0, :, sl], flat, exact_ties=True)
            for ref, val in zip(out_refs, outs_exact):
                ref[0, :, sl] = val

        return 0

    lax.fori_loop(0, chunks, chunk, 0)


def _peer_select(s1, s2, tl):
    h, n, t = s1.shape
    spec = pl.BlockSpec((1, n, tl), lambda hh, i: (hh, 0, i))
    wide = jax.ShapeDtypeStruct((h, n, t), F32)
    narrow = jax.ShapeDtypeStruct((h, n, t), BF16)
    tbl = jnp.asarray(_candidate_table())
    return pl.pallas_call(
        functools.partial(_peer_select_kernel, chunks=tl // SELECT_LANES),
        grid=(h, t // tl),
        in_specs=[spec, spec, pl.BlockSpec((_CAND_ROWS, SELECT_LANES), lambda hh, i: (0, 0))],
        out_specs=[spec] * 4,
        out_shape=[wide, wide, narrow, narrow],
        compiler_params=_cparams(2),
        name="peer_select",
    )(s1, s2, tbl)


def _gelu_tanh(x):
    return 0.5 * x * (1.0 + jnp.tanh(0.7978845608028654 * (x + 0.044715 * (x * x * x))))


def _zero_of(x):
    bits = lax.bitcast_convert_type(x, jnp.uint32)
    bits = lax.shift_right_logical(lax.shift_right_logical(bits, jnp.uint32(16)), jnp.uint32(16))
    return lax.bitcast_convert_type(bits, F32)


BF16_ROWS = 16


def _rows_bf16(row, n):
    tile = jnp.broadcast_to(row, (BF16_ROWS, row.shape[1])).astype(BF16)
    return jnp.tile(tile, (n // BF16_ROWS, 1))


def _peer_dense_kernel(h2t_ref, u_ref, v_ref, n1_ref, a1_ref, r2_ref, a2_ref, x1_ref, gate_ref, o_ref,
                       *a_scr, te, chunk):
    j = pl.program_id(1)

    @pl.when(j == 0)
    def _():
        o_ref[...] = jnp.zeros_like(o_ref)

    h2t = h2t_ref[...]
    zero = jnp.zeros((), BF16)
    per = chunk // N_KEYS
    n_chunks = te // chunk
    slot_w = lax.rem(j, 2)
    slot_r = lax.rem(j + 2, 2)

    def scores(q):
        a_scr[q][slot_w] = jnp.dot(u_ref[pl.ds(q * chunk, chunk), :], h2t, preferred_element_type=F32)

    scores(0)
    for q in range(n_chunks):
        if q + 1 < n_chunks:
            scores(q + 1)
        blocks = []
        for e in range(per):
            e1 = q * per + e
            pieces = []
            for c in range(h2t.shape[1] // LANES):
                cols = pl.ds(c * LANES, LANES)
                terms = []
                for h in range(PEER_HEADS):
                    n1 = _rows_bf16(n1_ref[h, pl.ds(e1, 1), cols], N_KEYS)
                    a1 = _rows_bf16(a1_ref[h, pl.ds(e1, 1), cols], N_KEYS)
                    keys = pl.ds(h * N_KEYS, N_KEYS)
                    chosen = jnp.maximum(n1 - r2_ref[keys, cols], zero)
                    terms.append(jnp.minimum(a1 * a2_ref[keys, cols], chosen))
                while len(terms) > 1:
                    terms = [terms[k] + terms[k + 1] for k in range(0, len(terms), 2)]
                a_t = a_scr[q][slot_r, pl.ds(e * N_KEYS, N_KEYS), cols]
                pieces.append(terms[0] * _gelu_tanh(a_t).astype(BF16))
            blocks.append(jnp.concatenate(pieces, axis=1))
        act = jnp.concatenate(blocks, axis=0)
        if q + 1 < n_chunks:
            act = act + _zero_of(a_scr[q + 1][slot_r, 0:1, :]).astype(BF16)
        o_ref[...] += _tn_dot(act, v_ref[pl.ds(q * chunk, chunk), :])

    @pl.when(j == pl.num_programs(1) - 1)
    def _():
        o_ref[...] = x1_ref[...] + gate_ref[...] * o_ref[...]


def _peer_dense(h2t, u, v, n1, a1, r2, a2, x1, gate2, tm, te):
    d, t = h2t.shape
    n_j = u.shape[0] // te
    e1_per_tile = te // N_KEYS
    single = pl.Buffered(1)
    small = pl.BlockSpec((PEER_HEADS, e1_per_tile, tm), lambda i, j: (0, j, i))
    big = pl.BlockSpec((PEER_HEADS * N_KEYS, tm), lambda i, j: (0, i), pipeline_mode=single)
    r2 = r2.reshape(PEER_HEADS * N_KEYS, t)
    a2 = a2.reshape(PEER_HEADS * N_KEYS, t)
    chunk = 2 * N_KEYS
    return pl.pallas_call(
        functools.partial(_peer_dense_kernel, te=te, chunk=chunk),
        grid=(t // tm, n_j),
        in_specs=[pl.BlockSpec((d, tm), lambda i, j: (0, i), pipeline_mode=single),
                  pl.BlockSpec((te, d), lambda i, j: (j, 0)),
                  pl.BlockSpec((te, d), lambda i, j: (j, 0)),
                  small, small, big, big,
                  pl.BlockSpec((tm, d), lambda i, j: (i, 0), pipeline_mode=single),
                  pl.BlockSpec((1, d), lambda i, j: (0, 0))],
        out_specs=pl.BlockSpec((tm, d), lambda i, j: (i, 0)),
        out_shape=jax.ShapeDtypeStruct((t, d), F32),
        scratch_shapes=[pltpu.VMEM((2, chunk, tm), F32)] * (te // chunk),
        compiler_params=_cparams(2),
        name="peer_dense",
    )(h2t, u, v, n1, a1, r2, a2, x1, gate2)


def _pad_cols(w, width):
    return jnp.pad(w, ((0, 0), (0, width - w.shape[1])))


def _head_pad_cols(w, per_head):
    k = w.shape[0]
    w = w.reshape(k, MLA_HEADS, per_head)
    w = jnp.pad(w, ((0, 0), (0, 0), (0, HEAD_PAD - per_head)))
    return w.reshape(k, MLA_HEADS * HEAD_PAD)


def _layer(x, mod, positions_tables, p, tiles):
    d = x.shape[1]
    shift1, scale1, gate1, shift2, scale2, gate2 = [mod[:, i * d:(i + 1) * d] for i in range(6)]
    cos_t, sin_t = positions_tables

    w_in = _pad_cols(p["w_in"], 2 * CONV_CH + Q_LORA + KV_LORA + LANES).astype(BF16)
    a_conv, q_lat, kv_lat, k_rope = _in_proj(x, p["norm1_g"], scale1, shift1, w_in, tiles["proj"])

    conv_out = _conv_module(a_conv, p["conv_dw_w"], p["conv_dw_b"], p["conv_ln_g"], p["conv_ln_b"],
                            tiles["conv"])

    wuq = _head_pad_cols(p["w_uq"], QK_HEAD).astype(BF16)
    gq = _pad_cols(p["q_norm_g"], HEAD_PAD)
    gk = _pad_cols(p["k_norm_g"], HEAD_PAD)
    q, k, vt = _mla_proj(q_lat, kv_lat, k_rope, cos_t, sin_t, p["q_a_norm_g"], p["kv_a_norm_g"],
                         wuq, p["w_ukv"].astype(BF16), gq, gk, tiles["proj"])
    attn_out = _attention(q, k, vt, tiles["attn"])

    x1, h2, h2t = _out_proj(x, conv_out, attn_out, p["w_out"].astype(BF16), gate1, p["norm2_g"], scale2,
                            shift2, tiles["proj"])

    s1, s2 = _peer_scores(h2, p["peer_wq"].astype(BF16), p["peer_k1"].astype(BF16),
                          p["peer_k2"].astype(BF16), tiles["proj"])
    n1, a1, r2, a2 = _peer_select(s1, s2, tiles["select"])
    return _peer_dense(h2t, p["peer_u"].astype(BF16), p["peer_v"].astype(BF16), n1, a1, r2, a2, x1, gate2,
                       tiles["dense_tokens"], tiles["dense_experts"])


def _tiles(s):
    return {"proj": min(512, s), "conv": min(256, s), "attn": min(512, s), "select": min(1024, s),
            "dense_tokens": min(512, s), "dense_experts": 1024}


def kernel(x, c, positions, w_ada, b_ada, norm1_g, w_in, conv_dw_w, conv_dw_b, conv_ln_g, conv_ln_b,
           q_a_norm_g, w_uq, kv_a_norm_g, w_ukv, q_norm_g, k_norm_g, w_out, norm2_g, peer_wq, peer_k1,
           peer_k2, peer_u, peer_v):
    b, s, d = x.shape
    assert b == 1, "kernel is written for batch 1"
    depth = w_ada.shape[0]
    row = lambda a, l: a[l].reshape(1, -1)
    tables = _rope_tables(positions[0])
    xs = x[0]
    for l in range(depth):
        mod = _ada_mod(c, w_ada[l], b_ada[l])
        params = {
            "norm1_g": row(norm1_g, l), "w_in": w_in[l], "conv_dw_w": conv_dw_w[l],
            "conv_dw_b": row(conv_dw_b, l), "conv_ln_g": row(conv_ln_g, l), "conv_ln_b": row(conv_ln_b, l),
            "q_a_norm_g": row(q_a_norm_g, l), "w_uq": w_uq[l], "kv_a_norm_g": row(kv_a_norm_g, l),
            "w_ukv": w_ukv[l], "q_norm_g": row(q_norm_g, l), "k_norm_g": row(k_norm_g, l),
            "w_out": w_out[l], "norm2_g": row(norm2_g, l), "peer_wq": peer_wq[l],
            "peer_k1": peer_k1[l], "peer_k2": peer_k2[l], "peer_u": peer_u[l], "peer_v": peer_v[l],
        }
        xs = _layer(xs, mod, tables, params, _tiles(s))
    return xs[None]
```

```python
import functools

import numpy as np
import jax
import jax.numpy as jnp
from jax import lax
from jax.experimental import pallas as pl
from jax.experimental.pallas import tpu as pltpu

F32 = jnp.float32
BF16 = jnp.bfloat16

CHUNK = 64
EPS = 1e-6
CONV_CH = 1024
CONV_WIDTH = 31
MLA_HEADS = 8
QK_NOPE = 128
QK_ROPE = 64
QK_HEAD = QK_NOPE + QK_ROPE
V_HEAD = 128
Q_LORA = 768
KV_LORA = 512
ROPE_THETA = 10000.0
PEER_HEADS = 8
N_KEYS = 128
PEER_TOPK = 16
PEER_HALF = 128

LANES = 128
SUBLANES = 8
HEAD_PAD = 256
CONV_HALO = 32
VMEM_LIMIT = 56 * 1024 * 1024

RANK_SCALE = 1024.0
NEG_INF = float("-inf")
LOG2_E = 1.4426950408889634


def _cparams(n_axes, vmem=VMEM_LIMIT, flags=None):
    return pltpu.CompilerParams(dimension_semantics=("arbitrary",) * n_axes, vmem_limit_bytes=vmem,
                                flags=flags)


def _nt_dot(a, b):
    return lax.dot_general(a, b, (((1,), (1,)), ((), ())), preferred_element_type=F32)


def _tn_dot(a, b):
    return lax.dot_general(a, b, (((0,), (0,)), ((), ())), preferred_element_type=F32)


def _ada_kernel(c_ref, w_ref, b_ref, o_ref):
    c = c_ref[...]
    ca = (c * jax.nn.sigmoid(c)).astype(BF16)
    o_ref[...] = jnp.dot(ca, w_ref[...].astype(BF16), preferred_element_type=F32) + b_ref[...]


def _ada_mod(c, w_ada, b_ada):
    d = c.shape[1]
    n = w_ada.shape[1]
    tn = 1024
    c8 = jnp.broadcast_to(c, (8, d))
    out = pl.pallas_call(
        _ada_kernel,
        grid=(n // tn,),
        in_specs=[pl.BlockSpec((8, d), lambda j: (0, 0)),
                  pl.BlockSpec((d, tn), lambda j: (0, j)),
                  pl.BlockSpec((1, tn), lambda j: (0, j))],
        out_specs=pl.BlockSpec((8, tn), lambda j: (0, j)),
        out_shape=jax.ShapeDtypeStruct((8, n), F32),
        compiler_params=_cparams(1),
        name="ada_mod",
    )(c8, w_ada, b_ada.reshape(1, n))
    return out[0:1]


def _rope_table_kernel(pos_ref, invf_ref, cos_ref, sin_ref):
    ang = pos_ref[...] * invf_ref[...]
    cos_ref[...] = jnp.cos(ang)
    sin_ref[...] = jnp.sin(ang)


def _rope_tables(positions):
    s = positions.shape[0]
    half = QK_ROPE // 2
    rep = LANES // half
    inv_freq = ROPE_THETA ** (-jnp.arange(0, QK_ROPE, 2, dtype=F32) / QK_ROPE)
    pos_rep = jnp.repeat(positions.astype(F32), half).reshape(s // rep, LANES)
    invf = jnp.tile(inv_freq, rep).reshape(1, LANES)
    cos_d, sin_d = pl.pallas_call(
        _rope_table_kernel,
        out_shape=(jax.ShapeDtypeStruct((s // rep, LANES), F32),) * 2,
        name="rope_table",
    )(pos_rep, invf)
    cos = cos_d.reshape(s, half)
    sin = sin_d.reshape(s, half)
    zeros = jnp.zeros((s, LANES - QK_ROPE), F32)
    return (jnp.concatenate([cos, cos, zeros], axis=1),
            jnp.concatenate([-sin, sin, zeros], axis=1))


def _inproj_kernel(x_ref, g_ref, sc_ref, sh_ref, w_ref, conv_ref, ql_ref, kvl_ref, kr_ref):
    x = x_ref[...]
    ms = jnp.mean(x * x, axis=-1, keepdims=True)
    h = x * lax.rsqrt(ms + EPS) * g_ref[...]
    h = h * (1.0 + sc_ref[...]) + sh_ref[...]
    a = jnp.dot(h.astype(BF16), w_ref[...], preferred_element_type=F32)
    o1 = 2 * CONV_CH
    o2 = o1 + Q_LORA
    o3 = o2 + KV_LORA
    conv_ref[...] = a[:, :o1]
    ql_ref[...] = a[:, o1:o2]
    kvl_ref[...] = a[:, o2:o3]
    kr_ref[...] = a[:, o3:o3 + LANES]


def _in_proj(x, g, scale, shift, w_in_pad, tm):
    s, d = x.shape
    n = w_in_pad.shape[1]
    row = lambda i: (i, 0)
    fixed = lambda i: (0, 0)
    return pl.pallas_call(
        _inproj_kernel,
        grid=(s // tm,),
        in_specs=[pl.BlockSpec((tm, d), row),
                  pl.BlockSpec((1, d), fixed), pl.BlockSpec((1, d), fixed), pl.BlockSpec((1, d), fixed),
                  pl.BlockSpec((d, n), fixed, pipeline_mode=pl.Buffered(1))],
        out_specs=[pl.BlockSpec((tm, 2 * CONV_CH), row), pl.BlockSpec((tm, Q_LORA), row),
                   pl.BlockSpec((tm, KV_LORA), row), pl.BlockSpec((tm, LANES), row)],
        out_shape=[jax.ShapeDtypeStruct((s, 2 * CONV_CH), F32), jax.ShapeDtypeStruct((s, Q_LORA), F32),
                   jax.ShapeDtypeStruct((s, KV_LORA), F32), jax.ShapeDtypeStruct((s, LANES), F32)],
        compiler_params=_cparams(1),
        name="in_proj",
    )(x, g, scale, shift, w_in_pad)


def _conv_kernel(cur_ref, halo_ref, w_ref, b_ref, g_ref, beta_ref, o_ref, hbuf, *, tm, rc):
    i = pl.program_id(0)
    cur = cur_ref[...]
    hbuf[pl.ds(CONV_HALO, tm), :] = cur[:, :CONV_CH] * jax.nn.sigmoid(cur[:, CONV_CH:])
    hal = halo_ref[...]
    hg = hal[:, :CONV_CH] * jax.nn.sigmoid(hal[:, CONV_CH:])
    hbuf[pl.ds(0, CONV_HALO), :] = jnp.where(i > 0, hg, 0.0)
    first = CONV_HALO - (CONV_WIDTH - 1)
    span = rc + CONV_HALO

    def row_chunk(r, _):
        r0 = pl.multiple_of(r * rc, rc)
        accs = []
        for c in range(CONV_CH // LANES):
            cols = pl.ds(c * LANES, LANES)
            x = hbuf[pl.ds(r0, span), cols]
            acc = jnp.zeros((rc, LANES), F32)
            for res in range(SUBLANES):
                taps = [k for k in range(CONV_WIDTH) if (first + k) % SUBLANES == res]
                xs = x if res == 0 else pltpu.roll(x, span - res, 0)
                for k in taps:
                    a = (first + k) // SUBLANES
                    acc = acc + w_ref[pl.ds(k, 1), cols] * xs[SUBLANES * a:SUBLANES * a + rc]
            accs.append(acc)
        acc = jnp.concatenate(accs, axis=1) + b_ref[...]
        mu = jnp.mean(acc, axis=-1, keepdims=True)
        cen = acc - mu
        var = jnp.mean(cen * cen, axis=-1, keepdims=True)
        y = cen * lax.rsqrt(var + EPS) * g_ref[...] + beta_ref[...]
        o_ref[pl.ds(r0, rc), :] = (y * jax.nn.sigmoid(y)).astype(BF16)
        return 0

    lax.fori_loop(0, tm // rc, row_chunk, 0)


def _conv_module(a_conv, dw_w, dw_b, ln_g, ln_b, tm):
    s = a_conv.shape[0]
    hb = tm // CONV_HALO
    fixed = lambda i: (0, 0)
    return pl.pallas_call(
        functools.partial(_conv_kernel, tm=tm, rc=32),
        grid=(s // tm,),
        in_specs=[pl.BlockSpec((tm, 2 * CONV_CH), lambda i: (i, 0)),
                  pl.BlockSpec((CONV_HALO, 2 * CONV_CH), lambda i: (jnp.maximum(i * hb - 1, 0), 0)),
                  pl.BlockSpec((CONV_WIDTH, CONV_CH), fixed),
                  pl.BlockSpec((1, CONV_CH), fixed), pl.BlockSpec((1, CONV_CH), fixed),
                  pl.BlockSpec((1, CONV_CH), fixed)],
        out_specs=pl.BlockSpec((tm, CONV_CH), lambda i: (i, 0)),
        out_shape=jax.ShapeDtypeStruct((s, CONV_CH), BF16),
        scratch_shapes=[pltpu.VMEM((tm + CONV_HALO, CONV_CH), F32)],
        compiler_params=_cparams(1),
        name="conv_module",
    )(a_conv, a_conv, dw_w, dw_b, ln_g, ln_b)


def _mla_proj_kernel(ql_ref, kvl_ref, kr_ref, cos_ref, sin_ref, gqa_ref, gkva_ref, wuq_ref, wukv_ref,
                     gq_ref, gk_ref, q_out, k_out, vt_out):
    def rms(v, g):
        return v * lax.rsqrt(jnp.mean(v * v, axis=-1, keepdims=True) + EPS) * g

    q = jnp.dot(rms(ql_ref[...], gqa_ref[...]).astype(BF16), wuq_ref[...], preferred_element_type=F32)
    kv = jnp.dot(rms(kvl_ref[...], gkva_ref[...]).astype(BF16), wukv_ref[...], preferred_element_type=F32)
    kr = kr_ref[...]
    cos = cos_ref[...]
    sin = sin_ref[...]
    lane = lax.broadcasted_iota(jnp.int32, kr.shape, 1)
    half = QK_ROPE // 2

    def rope(v):
        rot = jnp.where(lane < half, pltpu.roll(v, LANES - half, 1), pltpu.roll(v, half, 1))
        return v * cos + rot * sin

    gq = gq_ref[...]
    gk = gk_ref[...]
    scale = QK_HEAD ** -0.5 * LOG2_E
    ss_kr = jnp.sum(kr * kr, axis=-1, keepdims=True)
    for h in range(MLA_HEADS):
        qh = q[:, h * HEAD_PAD:(h + 1) * HEAD_PAD]
        rq = lax.rsqrt(jnp.sum(qh * qh, axis=-1, keepdims=True) / QK_HEAD + EPS)
        qh = qh * rq * gq
        q_out[h, :, 0:QK_NOPE] = (qh[:, :QK_NOPE] * scale).astype(BF16)
        q_out[h, :, QK_NOPE:HEAD_PAD] = (rope(qh[:, QK_NOPE:]) * scale).astype(BF16)
        kn = kv[:, h * HEAD_PAD:h * HEAD_PAD + QK_NOPE]
        vh = kv[:, h * HEAD_PAD + QK_NOPE:(h + 1) * HEAD_PAD]
        rk = lax.rsqrt((jnp.sum(kn * kn, axis=-1, keepdims=True) + ss_kr) / QK_HEAD + EPS)
        k_out[h, :, 0:QK_NOPE] = (kn * rk * gk[:, :QK_NOPE]).astype(BF16)
        k_out[h, :, QK_NOPE:HEAD_PAD] = rope(kr * rk * gk[:, QK_NOPE:]).astype(BF16)
        vt_out[h] = vh.T.astype(BF16)


def _mla_proj(q_lat, kv_lat, k_rope, cos_t, sin_t, gqa, gkva, wuq_pad, wukv, gq_pad, gk_pad, tm):
    s = q_lat.shape[0]
    row = lambda i: (i, 0)
    fixed = lambda i: (0, 0)
    hq = MLA_HEADS * HEAD_PAD
    return pl.pallas_call(
        _mla_proj_kernel,
        grid=(s // tm,),
        in_specs=[pl.BlockSpec((tm, Q_LORA), row), pl.BlockSpec((tm, KV_LORA), row),
                  pl.BlockSpec((tm, LANES), row), pl.BlockSpec((tm, LANES), row), pl.BlockSpec((tm, LANES), row),
                  pl.BlockSpec((1, Q_LORA), fixed), pl.BlockSpec((1, KV_LORA), fixed),
                  pl.BlockSpec((Q_LORA, hq), fixed), pl.BlockSpec((KV_LORA, hq), fixed),
                  pl.BlockSpec((1, HEAD_PAD), fixed), pl.BlockSpec((1, HEAD_PAD), fixed)],
        out_specs=[pl.BlockSpec((MLA_HEADS, tm, HEAD_PAD), lambda i: (0, i, 0)),
                   pl.BlockSpec((MLA_HEADS, tm, HEAD_PAD), lambda i: (0, i, 0)),
                   pl.BlockSpec((MLA_HEADS, V_HEAD, tm), lambda i: (0, 0, i))],
        out_shape=[jax.ShapeDtypeStruct((MLA_HEADS, s, HEAD_PAD), BF16),
                   jax.ShapeDtypeStruct((MLA_HEADS, s, HEAD_PAD), BF16),
                   jax.ShapeDtypeStruct((MLA_HEADS, V_HEAD, s), BF16)],
        compiler_params=_cparams(1),
        name="mla_proj",
    )(q_lat, kv_lat, k_rope, cos_t, sin_t, gqa, gkva, wuq_pad, wukv, gq_pad, gk_pad)


def _attn_kernel(q_ref, k_ref, vt_ref, u_ref, v_ref, o_ref, ub_ref, vb_ref, s_scr, mt_scr, m_scr, l_scr,
                 acc_scr, *, tq, heads):
    qi = pl.program_id(1)
    ub_ref[...] = u_ref[...].astype(BF16)
    vb_ref[...] = v_ref[...].astype(BF16)

    def produce(kt, slot, masked):
        k0 = pl.multiple_of(kt * tq, tq)
        for hh in range(heads):
            s = _nt_dot(k_ref[hh, pl.ds(k0, tq), :], q_ref[hh])
            if masked:
                kc = (k0 + lax.broadcasted_iota(jnp.int32, s.shape, 0)) // CHUNK
                qc = (qi * tq + lax.broadcasted_iota(jnp.int32, s.shape, 1)) // CHUNK
                s = jnp.where(kc <= qc, s, NEG_INF)
            s_scr[hh, slot] = s
            mt_scr[hh, slot] = jnp.max(s, axis=0, keepdims=True)

    def consume(kt, slot):
        k0 = pl.multiple_of(kt * tq, tq)
        for hh in range(heads):
            m = m_scr[hh]
            m_new = jnp.maximum(m, mt_scr[hh, slot])
            alpha = jnp.exp2(m - m_new)
            p = jnp.exp2(s_scr[hh, slot] - m_new)
            l_scr[hh] = alpha * l_scr[hh] + jnp.sum(p, axis=0, keepdims=True)
            pv = jnp.dot(vt_ref[hh, :, pl.ds(k0, tq)], p.astype(BF16), preferred_element_type=F32)
            acc_scr[hh] = alpha * acc_scr[hh] + pv
            m_scr[hh] = m_new

    m_scr[...] = jnp.full(m_scr.shape, NEG_INF, F32)
    l_scr[...] = jnp.zeros_like(l_scr)
    acc_scr[...] = jnp.zeros_like(acc_scr)

    @pl.when(qi == 0)
    def _():
        produce(0, 0, True)

    @pl.when(qi > 0)
    def _():
        produce(0, 0, False)

    def body(kt, _):
        slot = lax.rem(kt, 2)
        consume(kt, slot)
        produce(kt + 1, 1 - slot, False)
        return 0

    lax.fori_loop(0, jnp.maximum(qi - 1, 0), body, 0)

    @pl.when(qi > 0)
    def _():
        slot = lax.rem(qi - 1, 2)
        consume(qi - 1, slot)
        produce(qi, 1 - slot, True)

    consume(qi, lax.rem(qi, 2))
    for hh in range(heads):
        o_ref[:, hh * V_HEAD:(hh + 1) * V_HEAD] = (acc_scr[hh] / l_scr[hh]).T.astype(BF16)


def _attention(q, k, vt, peer_u, peer_v, tq, heads=2):
    h, s, _ = q.shape
    n_q = s // tq
    n_exp, d = peer_u.shape
    slab = n_exp // ((h // heads) * n_q)
    assert slab * (h // heads) * n_q == n_exp and slab % BF16_ROWS == 0
    table = pl.BlockSpec((slab, d), lambda g, i: (g * n_q + i, 0))
    return pl.pallas_call(
        functools.partial(_attn_kernel, tq=tq, heads=heads),
        grid=(h // heads, n_q),
        in_specs=[pl.BlockSpec((heads, tq, HEAD_PAD), lambda g, i: (g, i, 0)),
                  pl.BlockSpec((heads, s, HEAD_PAD), lambda g, i: (g, 0, 0)),
                  pl.BlockSpec((heads, V_HEAD, s), lambda g, i: (g, 0, 0)),
                  table, table],
        out_specs=[pl.BlockSpec((tq, heads * V_HEAD), lambda g, i: (i, g)), table, table],
        out_shape=[jax.ShapeDtypeStruct((s, h * V_HEAD), BF16),
                   jax.ShapeDtypeStruct((n_exp, d), BF16), jax.ShapeDtypeStruct((n_exp, d), BF16)],
        scratch_shapes=[pltpu.VMEM((heads, 2, tq, tq), F32), pltpu.VMEM((heads, 2, 1, tq), F32),
                        pltpu.VMEM((heads, 1, tq), F32), pltpu.VMEM((heads, 1, tq), F32),
                        pltpu.VMEM((heads, V_HEAD, tq), F32)],
        compiler_params=_cparams(2),
        name="attention",
    )(q, k, vt, peer_u, peer_v)


def _outproj_kernel(x_ref, conv_ref, attn_ref, w_ref, gate_ref, g_ref, sc_ref, sh_ref, x1_ref, h2_ref,
                    h2t_ref):
    mix = jnp.dot(conv_ref[...], w_ref[0:CONV_CH, :], preferred_element_type=F32)
    mix = mix + jnp.dot(attn_ref[...], w_ref[CONV_CH:, :], preferred_element_type=F32)
    x1 = x_ref[...] + gate_ref[...] * mix
    x1_ref[...] = x1
    ms = jnp.mean(x1 * x1, axis=-1, keepdims=True)
    h2 = x1 * lax.rsqrt(ms + EPS) * g_ref[...]
    h2 = h2 * (1.0 + sc_ref[...]) + sh_ref[...]
    h2_ref[...] = h2.astype(BF16)
    h2t_ref[...] = h2.T.astype(BF16)


def _out_proj(x, conv_out, attn_out, w_out, gate1, g2, scale2, shift2, tm):
    s, d = x.shape
    kdim = w_out.shape[0]
    row = lambda i: (i, 0)
    fixed = lambda i: (0, 0)
    return pl.pallas_call(
        _outproj_kernel,
        grid=(s // tm,),
        in_specs=[pl.BlockSpec((tm, d), row), pl.BlockSpec((tm, CONV_CH), row),
                  pl.BlockSpec((tm, kdim - CONV_CH), row),
                  pl.BlockSpec((kdim, d), fixed, pipeline_mode=pl.Buffered(1)),
                  pl.BlockSpec((1, d), fixed), pl.BlockSpec((1, d), fixed),
                  pl.BlockSpec((1, d), fixed), pl.BlockSpec((1, d), fixed)],
        out_specs=[pl.BlockSpec((tm, d), row), pl.BlockSpec((tm, d), row),
                   pl.BlockSpec((d, tm), lambda i: (0, i))],
        out_shape=[jax.ShapeDtypeStruct((s, d), F32), jax.ShapeDtypeStruct((s, d), BF16),
                   jax.ShapeDtypeStruct((d, s), BF16)],
        compiler_params=_cparams(1),
        name="out_proj",
    )(x, conv_out, attn_out, w_out, gate1, g2, scale2, shift2)


def _peer_scores_kernel(h2_ref, wq_ref, k1_ref, k2_ref, s1_ref, s2_ref):
    q = jnp.dot(h2_ref[...], wq_ref[...], preferred_element_type=F32).astype(BF16)
    for h in range(PEER_HEADS):
        base = h * 2 * PEER_HALF
        s1_ref[h] = _nt_dot(k1_ref[h], q[:, base:base + PEER_HALF])
        s2_ref[h] = _nt_dot(k2_ref[h], q[:, base + PEER_HALF:base + 2 * PEER_HALF])


def _peer_scores(h2, wq, k1, k2, tm):
    t, d = h2.shape
    spec = pl.BlockSpec((PEER_HEADS, N_KEYS, tm), lambda i: (0, 0, i))
    keys = pl.BlockSpec((PEER_HEADS, N_KEYS, PEER_HALF), lambda i: (0, 0, 0))
    shape = jax.ShapeDtypeStruct((PEER_HEADS, N_KEYS, t), F32)
    return pl.pallas_call(
        _peer_scores_kernel,
        grid=(t // tm,),
        in_specs=[pl.BlockSpec((tm, d), lambda i: (i, 0)),
                  pl.BlockSpec(wq.shape, lambda i: (0, 0), pipeline_mode=pl.Buffered(1)),
                  keys, keys],
        out_specs=[spec, spec],
        out_shape=[shape, shape],
        compiler_params=_cparams(1),
        name="peer_scores",
    )(h2, wq, k1, k2)


_CAND_ROWS = 80
SELECT_LANES = 256


def _candidate_table():
    tbl = np.full((_CAND_ROWS,), -1, np.int32)
    for i in range(PEER_TOPK):
        tbl[i] = i * PEER_TOPK
    for j in range(1, 8):
        for i in range(8):
            if (i + 1) * (j + 1) <= PEER_TOPK:
                tbl[16 + 8 * (j - 1) + i] = i * PEER_TOPK + j
    for j in range(8, PEER_TOPK):
        tbl[72 + j - 8] = j
    return np.broadcast_to(tbl[:, None], (_CAND_ROWS, SELECT_LANES)).copy()


def _top16_rows(s, exact_ties):
    row = lax.broadcasted_iota(jnp.int32, s.shape, 0) if exact_ties else None
    rank = jnp.full(s.shape, float(PEER_TOPK), F32)
    vals = []
    for r in range(PEER_TOPK):
        m = jnp.max(s, axis=0, keepdims=True)
        if exact_ties:
            first = jnp.min(jnp.where(s == m, row, N_KEYS), axis=0, keepdims=True)
            sel = row == first
        else:
            sel = s == m
        rank = jnp.where(sel, float(r), rank)
        s = jnp.where(sel, NEG_INF, s)
        vals.append(m)
    return jnp.concatenate(vals, axis=0), rank


def _count_off(marked):
    cnt = jnp.sum(jnp.where(marked, 1.0, 0.0), axis=0, keepdims=True)
    return jnp.where(cnt != float(PEER_TOPK), 1.0, 0.0)


def _select_chunk(s1, s2, flat, exact_ties):
    valid = flat >= 0
    row8 = lax.broadcasted_iota(jnp.int32, (8, SELECT_LANES), 0)
    v1, rank1 = _top16_rows(s1, exact_ties)
    v2, rank2 = _top16_rows(s2, exact_ties)
    blocks = [v1 + v2[0:1]]
    for j in range(1, 8):
        blocks.append(v1[0:8] + v2[j:j + 1])
    blocks.append(v1[0:1] + v2[8:16])
    cand0 = jnp.where(valid, jnp.concatenate(blocks, axis=0), NEG_INF)
    cand = cand0
    chosen = jnp.zeros(cand.shape, F32)
    for _ in range(PEER_TOPK):
        m = jnp.max(cand, axis=0, keepdims=True)
        if exact_ties:
            first = jnp.min(jnp.where(cand == m, flat, PEER_TOPK * PEER_TOPK), axis=0, keepdims=True)
            sel = flat == first
        else:
            sel = cand == m
        chosen = jnp.where(sel, 1.0, chosen)
        cand = jnp.where(sel, NEG_INF, cand)
    n_lo = chosen[0:8]
    for j in range(1, 8):
        n_lo = n_lo + chosen[16 + 8 * (j - 1):16 + 8 * j]
    tail = jnp.sum(chosen[72:80], axis=0, keepdims=True)
    n_lo = n_lo + jnp.where(row8 == 0, tail, 0.0)
    n = jnp.concatenate([n_lo, chosen[8:16]], axis=0)
    top = v1[0:1] + v2[0:1]
    z = jnp.sum(jnp.where(chosen > 0.0, jnp.exp(cand0 - top), 0.0), axis=0, keepdims=True)
    n1 = jnp.zeros(s1.shape, F32)
    for r in range(PEER_TOPK):
        n1 = jnp.where(rank1 == float(r), n[r:r + 1], n1)
    off = (_count_off(rank1 < float(PEER_TOPK)) + _count_off(rank2 < float(PEER_TOPK))
           + _count_off(chosen > 0.0))
    outs = (n1 * RANK_SCALE, jnp.exp(s1 - v1[0:1]), (rank2 * RANK_SCALE).astype(BF16),
            (jnp.exp(s2 - v2[0:1]) / z).astype(BF16))
    return outs, off


def _peer_select_kernel(s1_ref, s2_ref, tbl_ref, n1_ref, a1_ref, r2_ref, a2_ref, *, chunks):
    flat = tbl_ref[...]
    out_refs = (n1_ref, a1_ref, r2_ref, a2_ref)

    def chunk(c, _):
        sl = pl.ds(pl.multiple_of(c * SELECT_LANES, SELECT_LANES), SELECT_LANES)
        s1 = s1_ref[0, :, sl]
        s2 = s2_ref[0, :, sl]
        outs, off = _select_chunk(s1, s2, flat, exact_ties=False)
        for ref, val in zip(out_refs, outs):
            ref[0, :, sl] = val

        @pl.when(jnp.sum(off) > 0.0)
        def _():
            outs_exact, _ = _select_chunk(s1_ref[0, :, sl], s2_ref[0, :, sl], flat, exact_ties=True)
            for ref, val in zip(out_refs, outs_exact):
                ref[0, :, sl] = val

        return 0

    lax.fori_loop(0, chunks, chunk, 0)


def _peer_select(s1, s2, tl):
    h, n, t = s1.shape
    spec = pl.BlockSpec((1, n, tl), lambda hh, i: (hh, 0, i))
    wide = jax.ShapeDtypeStruct((h, n, t), F32)
    narrow = jax.ShapeDtypeStruct((h, n, t), BF16)
    tbl = jnp.asarray(_candidate_table())
    return pl.pallas_call(
        functools.partial(_peer_select_kernel, chunks=tl // SELECT_LANES),
        grid=(h, t // tl),
        in_specs=[spec, spec, pl.BlockSpec((_CAND_ROWS, SELECT_LANES), lambda hh, i: (0, 0))],
        out_specs=[spec] * 4,
        out_shape=[wide, wide, narrow, narrow],
        compiler_params=_cparams(2),
        name="peer_select",
    )(s1, s2, tbl)


def _gelu_tanh(x):
    return 0.5 * x * (1.0 + jnp.tanh(0.7978845608028654 * (x + 0.044715 * (x * x * x))))


def _zero_of(x):
    bits = lax.bitcast_convert_type(x, jnp.uint32)
    bits = lax.shift_right_logical(lax.shift_right_logical(bits, jnp.uint32(16)), jnp.uint32(16))
    return lax.bitcast_convert_type(bits, F32)


BF16_ROWS = 16


def _rows_bf16(row, n):
    tile = jnp.broadcast_to(row, (BF16_ROWS, row.shape[1])).astype(BF16)
    return jnp.tile(tile, (n // BF16_ROWS, 1))


def _peer_dense_kernel(h2t_ref, u_ref, v_ref, n1_ref, a1_ref, r2_ref, a2_ref, x1_ref, gate_ref, o_ref,
                       *a_scr, te, chunk):
    j = pl.program_id(1)

    @pl.when(j == 0)
    def _():
        o_ref[...] = jnp.zeros_like(o_ref)

    h2t = h2t_ref[...]
    zero = jnp.zeros((), BF16)
    per = chunk // N_KEYS
    n_chunks = te // chunk
    slot_w = lax.rem(j, 2)
    slot_r = lax.rem(j + 2, 2)

    def scores(q):
        a_scr[q][slot_w] = jnp.dot(u_ref[pl.ds(q * chunk, chunk), :], h2t, preferred_element_type=F32)

    scores(0)
    for q in range(n_chunks):
        if q + 1 < n_chunks:
            scores(q + 1)
        blocks = []
        for e in range(per):
            e1 = q * per + e
            pieces = []
            for c in range(h2t.shape[1] // LANES):
                cols = pl.ds(c * LANES, LANES)
                terms = []
                for h in range(PEER_HEADS):
                    n1 = _rows_bf16(n1_ref[h, pl.ds(e1, 1), cols], N_KEYS)
                    a1 = _rows_bf16(a1_ref[h, pl.ds(e1, 1), cols], N_KEYS)
                    keys = pl.ds(h * N_KEYS, N_KEYS)
                    chosen = jnp.maximum(n1 - r2_ref[keys, cols], zero)
                    terms.append(jnp.minimum(a1 * a2_ref[keys, cols], chosen))
                while len(terms) > 1:
                    terms = [terms[k] + terms[k + 1] for k in range(0, len(terms), 2)]
                a_t = a_scr[q][slot_r, pl.ds(e * N_KEYS, N_KEYS), cols]
                pieces.append(terms[0] * _gelu_tanh(a_t.astype(BF16)))
            blocks.append(jnp.concatenate(pieces, axis=1))
        act = jnp.concatenate(blocks, axis=0)
        if q + 1 < n_chunks:
            act = act + _zero_of(a_scr[q + 1][slot_r, 0:1, :]).astype(BF16)
        o_ref[...] += _tn_dot(act, v_ref[pl.ds(q * chunk, chunk), :])

    @pl.when(j == pl.num_programs(1) - 1)
    def _():
        o_ref[...] = x1_ref[...] + gate_ref[...] * o_ref[...]


def _peer_dense(h2t, u, v, n1, a1, r2, a2, x1, gate2, tm, te):
    d, t = h2t.shape
    n_j = u.shape[0] // te
    e1_per_tile = te // N_KEYS
    single = pl.Buffered(1)
    small = pl.BlockSpec((PEER_HEADS, e1_per_tile, tm), lambda i, j: (0, j, i))
    big = pl.BlockSpec((PEER_HEADS * N_KEYS, tm), lambda i, j: (0, i), pipeline_mode=single)
    r2 = r2.reshape(PEER_HEADS * N_KEYS, t)
    a2 = a2.reshape(PEER_HEADS * N_KEYS, t)
    chunk = 4 * N_KEYS
    return pl.pallas_call(
        functools.partial(_peer_dense_kernel, te=te, chunk=chunk),
        grid=(t // tm, n_j),
        in_specs=[pl.BlockSpec((d, tm), lambda i, j: (0, i), pipeline_mode=single),
                  pl.BlockSpec((te, d), lambda i, j: (j, 0)),
                  pl.BlockSpec((te, d), lambda i, j: (j, 0)),
                  small, small, big, big,
                  pl.BlockSpec((tm, d), lambda i, j: (i, 0), pipeline_mode=single),
                  pl.BlockSpec((1, d), lambda i, j: (0, 0))],
        out_specs=pl.BlockSpec((tm, d), lambda i, j: (i, 0)),
        out_shape=jax.ShapeDtypeStruct((t, d), F32),
        scratch_shapes=[pltpu.VMEM((2, chunk, tm), F32)] * (te // chunk),
        compiler_params=_cparams(2),
        name="peer_dense",
    )(h2t, u, v, n1, a1, r2, a2, x1, gate2)


def _pad_cols(w, width):
    return jnp.pad(w, ((0, 0), (0, width - w.shape[1])))


def _head_pad_cols(w, per_head):
    k = w.shape[0]
    w = w.reshape(k, MLA_HEADS, per_head)
    w = jnp.pad(w, ((0, 0), (0, 0), (0, HEAD_PAD - per_head)))
    return w.reshape(k, MLA_HEADS * HEAD_PAD)


def _layer(x, mod, positions_tables, p, tiles):
    d = x.shape[1]
    shift1, scale1, gate1, shift2, scale2, gate2 = [mod[:, i * d:(i + 1) * d] for i in range(6)]
    cos_t, sin_t = positions_tables

    w_in = _pad_cols(p["w_in"], 2 * CONV_CH + Q_LORA + KV_LORA + LANES).astype(BF16)
    a_conv, q_lat, kv_lat, k_rope = _in_proj(x, p["norm1_g"], scale1, shift1, w_in, tiles["proj"])

    conv_out = _conv_module(a_conv, p["conv_dw_w"], p["conv_dw_b"], p["conv_ln_g"], p["conv_ln_b"],
                            tiles["conv"])

    wuq = _head_pad_cols(p["w_uq"], QK_HEAD).astype(BF16)
    gq = _pad_cols(p["q_norm_g"], HEAD_PAD)
    gk = _pad_cols(p["k_norm_g"], HEAD_PAD)
    q, k, vt = _mla_proj(q_lat, kv_lat, k_rope, cos_t, sin_t, p["q_a_norm_g"], p["kv_a_norm_g"],
                         wuq, p["w_ukv"].astype(BF16), gq, gk, tiles["proj"])
    attn_out, u_bf16, v_bf16 = _attention(q, k, vt, p["peer_u"], p["peer_v"], tiles["attn"])

    x1, h2, h2t = _out_proj(x, conv_out, attn_out, p["w_out"].astype(BF16), gate1, p["norm2_g"], scale2,
                            shift2, tiles["proj"])

    s1, s2 = _peer_scores(h2, p["peer_wq"].astype(BF16), p["peer_k1"].astype(BF16),
                          p["peer_k2"].astype(BF16), tiles["proj"])
    n1, a1, r2, a2 = _peer_select(s1, s2, tiles["select"])
    return _peer_dense(h2t, u_bf16, v_bf16, n1, a1, r2, a2, x1, gate2,
                       tiles["dense_tokens"], tiles["dense_experts"])


def _tiles(s):
    return {"proj": min(512, s), "conv": min(256, s), "attn": min(512, s), "select": min(1024, s),
            "dense_tokens": min(512, s), "dense_experts": 1024}


def kernel(x, c, positions, w_ada, b_ada, norm1_g, w_in, conv_dw_w, conv_dw_b, conv_ln_g, conv_ln_b,
           q_a_norm_g, w_uq, kv_a_norm_g, w_ukv, q_norm_g, k_norm_g, w_out, norm2_g, peer_wq, peer_k1,
           peer_k2, peer_u, peer_v):
    b, s, d = x.shape
    assert b == 1, "kernel is written for batch 1"
    depth = w_ada.shape[0]
    row = lambda a, l: a[l].reshape(1, -1)
    tables = _rope_tables(positions[0])
    xs = x[0]
    for l in range(depth):
        mod = _ada_mod(c, w_ada[l], b_ada[l])
        params = {
            "norm1_g": row(norm1_g, l), "w_in": w_in[l], "conv_dw_w": conv_dw_w[l],
            "conv_dw_b": row(conv_dw_b, l), "conv_ln_g": row(conv_ln_g, l), "conv_ln_b": row(conv_ln_b, l),
            "q_a_norm_g": row(q_a_norm_g, l), "w_uq": w_uq[l], "kv_a_norm_g": row(kv_a_norm_g, l),
            "w_ukv": w_ukv[l], "q_norm_g": row(q_norm_g, l), "k_norm_g": row(k_norm_g, l),
            "w_out": w_out[l], "norm2_g": row(norm2_g, l), "peer_wq": peer_wq[l],
            "peer_k1": peer_k1[l], "peer_k2": peer_k2[l], "peer_u": peer_u[l], "peer_v": peer_v[l],
        }
        xs = _layer(xs, mod, tables, params, _tiles(s))
    return xs[None]
```

```python
import functools

import numpy as np
import jax
import jax.numpy as jnp
from jax import lax
from jax.experimental import pallas as pl
from jax.experimental.pallas import tpu as pltpu

F32 = jnp.float32
BF16 = jnp.bfloat16

CHUNK = 64
EPS = 1e-6
CONV_CH = 1024
CONV_WIDTH = 31
MLA_HEADS = 8
QK_NOPE = 128
QK_ROPE = 64
QK_HEAD = QK_NOPE + QK_ROPE
V_HEAD = 128
Q_LORA = 768
KV_LORA = 512
ROPE_THETA = 10000.0
PEER_HEADS = 8
N_KEYS = 128
PEER_TOPK = 16
PEER_HALF = 128

LANES = 128
SUBLANES = 8
HEAD_PAD = 256
CONV_HALO = 32
VMEM_LIMIT = 56 * 1024 * 1024

RANK_SCALE = 1024.0
NEG_INF = float("-inf")
LOG2_E = 1.4426950408889634


def _cparams(n_axes, vmem=VMEM_LIMIT, flags=None):
    return pltpu.CompilerParams(dimension_semantics=("arbitrary",) * n_axes, vmem_limit_bytes=vmem,
                                flags=flags)


def _nt_dot(a, b):
    return lax.dot_general(a, b, (((1,), (1,)), ((), ())), preferred_element_type=F32)


def _tn_dot(a, b):
    return lax.dot_general(a, b, (((0,), (0,)), ((), ())), preferred_element_type=F32)


def _ada_kernel(c_ref, w_ref, b_ref, o_ref):
    c = c_ref[...]
    ca = (c * jax.nn.sigmoid(c)).astype(BF16)
    o_ref[...] = jnp.dot(ca, w_ref[...].astype(BF16), preferred_element_type=F32) + b_ref[...]


def _ada_mod(c, w_ada, b_ada):
    d = c.shape[1]
    n = w_ada.shape[1]
    tn = 1024
    c8 = jnp.broadcast_to(c, (8, d))
    out = pl.pallas_call(
        _ada_kernel,
        grid=(n // tn,),
        in_specs=[pl.BlockSpec((8, d), lambda j: (0, 0)),
                  pl.BlockSpec((d, tn), lambda j: (0, j)),
                  pl.BlockSpec((1, tn), lambda j: (0, j))],
        out_specs=pl.BlockSpec((8, tn), lambda j: (0, j)),
        out_shape=jax.ShapeDtypeStruct((8, n), F32),
        compiler_params=_cparams(1),
        name="ada_mod",
    )(c8, w_ada, b_ada.reshape(1, n))
    return out[0:1]


def _rope_table_kernel(pos_ref, invf_ref, cos_ref, sin_ref):
    ang = pos_ref[...] * invf_ref[...]
    cos_ref[...] = jnp.cos(ang)
    sin_ref[...] = jnp.sin(ang)


def _rope_tables(positions):
    s = positions.shape[0]
    half = QK_ROPE // 2
    rep = LANES // half
    inv_freq = ROPE_THETA ** (-jnp.arange(0, QK_ROPE, 2, dtype=F32) / QK_ROPE)
    pos_rep = jnp.repeat(positions.astype(F32), half).reshape(s // rep, LANES)
    invf = jnp.tile(inv_freq, rep).reshape(1, LANES)
    cos_d, sin_d = pl.pallas_call(
        _rope_table_kernel,
        out_shape=(jax.ShapeDtypeStruct((s // rep, LANES), F32),) * 2,
        name="rope_table",
    )(pos_rep, invf)
    cos = cos_d.reshape(s, half)
    sin = sin_d.reshape(s, half)
    zeros = jnp.zeros((s, LANES - QK_ROPE), F32)
    return (jnp.concatenate([cos, cos, zeros], axis=1),
            jnp.concatenate([-sin, sin, zeros], axis=1))


def _inproj_kernel(x_ref, g_ref, sc_ref, sh_ref, w_ref, wkr_ref, conv_ref, ql_ref, kvl_ref, kr_ref):
    x = x_ref[...]
    ms = jnp.mean(x * x, axis=-1, keepdims=True)
    h = x * lax.rsqrt(ms + EPS) * g_ref[...]
    h = (h * (1.0 + sc_ref[...]) + sh_ref[...]).astype(BF16)
    a = jnp.dot(h, w_ref[...], preferred_element_type=F32)
    o1 = 2 * CONV_CH
    o2 = o1 + Q_LORA
    conv_ref[...] = a[:, :o1]
    ql_ref[...] = a[:, o1:o2]
    kvl_ref[...] = a[:, o2:]
    kr_ref[...] = jnp.dot(h, wkr_ref[...], preferred_element_type=F32)


def _in_proj(x, g, scale, shift, w_main, w_kr, tm):
    s, d = x.shape
    n = w_main.shape[1]
    row = lambda i: (i, 0)
    fixed = lambda i: (0, 0)
    return pl.pallas_call(
        _inproj_kernel,
        grid=(s // tm,),
        in_specs=[pl.BlockSpec((tm, d), row),
                  pl.BlockSpec((1, d), fixed), pl.BlockSpec((1, d), fixed), pl.BlockSpec((1, d), fixed),
                  pl.BlockSpec((d, n), fixed, pipeline_mode=pl.Buffered(1)),
                  pl.BlockSpec((d, LANES), fixed)],
        out_specs=[pl.BlockSpec((tm, 2 * CONV_CH), row), pl.BlockSpec((tm, Q_LORA), row),
                   pl.BlockSpec((tm, KV_LORA), row), pl.BlockSpec((tm, LANES), row)],
        out_shape=[jax.ShapeDtypeStruct((s, 2 * CONV_CH), F32), jax.ShapeDtypeStruct((s, Q_LORA), F32),
                   jax.ShapeDtypeStruct((s, KV_LORA), F32), jax.ShapeDtypeStruct((s, LANES), F32)],
        compiler_params=_cparams(1),
        name="in_proj",
    )(x, g, scale, shift, w_main, w_kr)


def _conv_kernel(cur_ref, halo_ref, w_ref, b_ref, g_ref, beta_ref, o_ref, hbuf, *, tm, rc):
    i = pl.program_id(0)
    cur = cur_ref[...]
    hbuf[pl.ds(CONV_HALO, tm), :] = cur[:, :CONV_CH] * jax.nn.sigmoid(cur[:, CONV_CH:])
    hal = halo_ref[...]
    hg = hal[:, :CONV_CH] * jax.nn.sigmoid(hal[:, CONV_CH:])
    hbuf[pl.ds(0, CONV_HALO), :] = jnp.where(i > 0, hg, 0.0)
    first = CONV_HALO - (CONV_WIDTH - 1)
    span = rc + CONV_HALO

    def row_chunk(r, _):
        r0 = pl.multiple_of(r * rc, rc)
        accs = []
        for c in range(CONV_CH // LANES):
            cols = pl.ds(c * LANES, LANES)
            x = hbuf[pl.ds(r0, span), cols]
            acc = jnp.zeros((rc, LANES), F32)
            for res in range(SUBLANES):
                taps = [k for k in range(CONV_WIDTH) if (first + k) % SUBLANES == res]
                xs = x if res == 0 else pltpu.roll(x, span - res, 0)
                for k in taps:
                    a = (first + k) // SUBLANES
                    acc = acc + w_ref[pl.ds(k, 1), cols] * xs[SUBLANES * a:SUBLANES * a + rc]
            accs.append(acc)
        acc = jnp.concatenate(accs, axis=1) + b_ref[...]
        mu = jnp.mean(acc, axis=-1, keepdims=True)
        cen = acc - mu
        var = jnp.mean(cen * cen, axis=-1, keepdims=True)
        y = cen * lax.rsqrt(var + EPS) * g_ref[...] + beta_ref[...]
        o_ref[pl.ds(r0, rc), :] = (y * jax.nn.sigmoid(y)).astype(BF16)
        return 0

    lax.fori_loop(0, tm // rc, row_chunk, 0)


def _conv_module(a_conv, dw_w, dw_b, ln_g, ln_b, tm):
    s = a_conv.shape[0]
    hb = tm // CONV_HALO
    fixed = lambda i: (0, 0)
    return pl.pallas_call(
        functools.partial(_conv_kernel, tm=tm, rc=32),
        grid=(s // tm,),
        in_specs=[pl.BlockSpec((tm, 2 * CONV_CH), lambda i: (i, 0)),
                  pl.BlockSpec((CONV_HALO, 2 * CONV_CH), lambda i: (jnp.maximum(i * hb - 1, 0), 0)),
                  pl.BlockSpec((CONV_WIDTH, CONV_CH), fixed),
                  pl.BlockSpec((1, CONV_CH), fixed), pl.BlockSpec((1, CONV_CH), fixed),
                  pl.BlockSpec((1, CONV_CH), fixed)],
        out_specs=pl.BlockSpec((tm, CONV_CH), lambda i: (i, 0)),
        out_shape=jax.ShapeDtypeStruct((s, CONV_CH), BF16),
        scratch_shapes=[pltpu.VMEM((tm + CONV_HALO, CONV_CH), F32)],
        compiler_params=_cparams(1),
        name="conv_module",
    )(a_conv, a_conv, dw_w, dw_b, ln_g, ln_b)


def _mla_proj_kernel(ql_ref, kvl_ref, kr_ref, cos_ref, sin_ref, gqa_ref, gkva_ref, wuq_ref, wukv_ref,
                     gq_ref, gk_ref, q_out, k_out, vt_out):
    def rms(v, g):
        return v * lax.rsqrt(jnp.mean(v * v, axis=-1, keepdims=True) + EPS) * g

    q = jnp.dot(rms(ql_ref[...], gqa_ref[...]).astype(BF16), wuq_ref[...], preferred_element_type=F32)
    kv = jnp.dot(rms(kvl_ref[...], gkva_ref[...]).astype(BF16), wukv_ref[...], preferred_element_type=F32)
    kr = kr_ref[...]
    cos = cos_ref[...]
    sin = sin_ref[...]
    lane = lax.broadcasted_iota(jnp.int32, kr.shape, 1)
    half = QK_ROPE // 2

    def rope(v):
        rot = jnp.where(lane < half, pltpu.roll(v, LANES - half, 1), pltpu.roll(v, half, 1))
        return v * cos + rot * sin

    gq = gq_ref[...]
    gk = gk_ref[...]
    scale = QK_HEAD ** -0.5 * LOG2_E
    ss_kr = jnp.sum(kr * kr, axis=-1, keepdims=True)
    for h in range(MLA_HEADS):
        qh = q[:, h * HEAD_PAD:(h + 1) * HEAD_PAD]
        rq = lax.rsqrt(jnp.sum(qh * qh, axis=-1, keepdims=True) / QK_HEAD + EPS)
        qh = qh * rq * gq
        q_out[h, :, 0:QK_NOPE] = (qh[:, :QK_NOPE] * scale).astype(BF16)
        q_out[h, :, QK_NOPE:HEAD_PAD] = (rope(qh[:, QK_NOPE:]) * scale).astype(BF16)
        kn = kv[:, h * HEAD_PAD:h * HEAD_PAD + QK_NOPE]
        vh = kv[:, h * HEAD_PAD + QK_NOPE:(h + 1) * HEAD_PAD]
        rk = lax.rsqrt((jnp.sum(kn * kn, axis=-1, keepdims=True) + ss_kr) / QK_HEAD + EPS)
        k_out[h, :, 0:QK_NOPE] = (kn * rk * gk[:, :QK_NOPE]).astype(BF16)
        k_out[h, :, QK_NOPE:HEAD_PAD] = rope(kr * rk * gk[:, QK_NOPE:]).astype(BF16)
        vt_out[h] = vh.T.astype(BF16)


def _mla_proj(q_lat, kv_lat, k_rope, cos_t, sin_t, gqa, gkva, wuq_pad, wukv, gq_pad, gk_pad, tm):
    s = q_lat.shape[0]
    row = lambda i: (i, 0)
    fixed = lambda i: (0, 0)
    hq = MLA_HEADS * HEAD_PAD
    return pl.pallas_call(
        _mla_proj_kernel,
        grid=(s // tm,),
        in_specs=[pl.BlockSpec((tm, Q_LORA), row), pl.BlockSpec((tm, KV_LORA), row),
                  pl.BlockSpec((tm, LANES), row), pl.BlockSpec((tm, LANES), row), pl.BlockSpec((tm, LANES), row),
                  pl.BlockSpec((1, Q_LORA), fixed), pl.BlockSpec((1, KV_LORA), fixed),
                  pl.BlockSpec((Q_LORA, hq), fixed), pl.BlockSpec((KV_LORA, hq), fixed),
                  pl.BlockSpec((1, HEAD_PAD), fixed), pl.BlockSpec((1, HEAD_PAD), fixed)],
        out_specs=[pl.BlockSpec((MLA_HEADS, tm, HEAD_PAD), lambda i: (0, i, 0)),
                   pl.BlockSpec((MLA_HEADS, tm, HEAD_PAD), lambda i: (0, i, 0)),
                   pl.BlockSpec((MLA_HEADS, V_HEAD, tm), lambda i: (0, 0, i))],
        out_shape=[jax.ShapeDtypeStruct((MLA_HEADS, s, HEAD_PAD), BF16),
                   jax.ShapeDtypeStruct((MLA_HEADS, s, HEAD_PAD), BF16),
                   jax.ShapeDtypeStruct((MLA_HEADS, V_HEAD, s), BF16)],
        compiler_params=_cparams(1),
        name="mla_proj",
    )(q_lat, kv_lat, k_rope, cos_t, sin_t, gqa, gkva, wuq_pad, wukv, gq_pad, gk_pad)


def _attn_kernel(q_ref, k_ref, vt_ref, u_ref, v_ref, o_ref, ub_ref, vb_ref, s_scr, mt_scr, m_scr, l_scr,
                 acc_scr, *, tq, heads):
    qi = pl.program_id(1)
    ub_ref[...] = u_ref[...].astype(BF16)
    vb_ref[...] = v_ref[...].astype(BF16)

    def produce(kt, slot, masked):
        k0 = pl.multiple_of(kt * tq, tq)
        for hh in range(heads):
            s = _nt_dot(k_ref[hh, pl.ds(k0, tq), :], q_ref[hh])
            if masked:
                kc = (k0 + lax.broadcasted_iota(jnp.int32, s.shape, 0)) // CHUNK
                qc = (qi * tq + lax.broadcasted_iota(jnp.int32, s.shape, 1)) // CHUNK
                s = jnp.where(kc <= qc, s, NEG_INF)
            s_scr[hh, slot] = s
            mt_scr[hh, slot] = jnp.max(s, axis=0, keepdims=True)

    def consume(kt, slot):
        k0 = pl.multiple_of(kt * tq, tq)
        for hh in range(heads):
            m = m_scr[hh]
            m_new = jnp.maximum(m, mt_scr[hh, slot])
            alpha = jnp.exp2(m - m_new)
            p = jnp.exp2(s_scr[hh, slot] - m_new)
            l_scr[hh] = alpha * l_scr[hh] + jnp.sum(p, axis=0, keepdims=True)
            pv = jnp.dot(vt_ref[hh, :, pl.ds(k0, tq)], p.astype(BF16), preferred_element_type=F32)
            acc_scr[hh] = alpha * acc_scr[hh] + pv
            m_scr[hh] = m_new

    m_scr[...] = jnp.full(m_scr.shape, NEG_INF, F32)
    l_scr[...] = jnp.zeros_like(l_scr)
    acc_scr[...] = jnp.zeros_like(acc_scr)

    @pl.when(qi == 0)
    def _():
        produce(0, 0, True)

    @pl.when(qi > 0)
    def _():
        produce(0, 0, False)

    def body(kt, _):
        slot = lax.rem(kt, 2)
        consume(kt, slot)
        produce(kt + 1, 1 - slot, False)
        return 0

    lax.fori_loop(0, jnp.maximum(qi - 1, 0), body, 0)

    @pl.when(qi > 0)
    def _():
        slot = lax.rem(qi - 1, 2)
        consume(qi - 1, slot)
        produce(qi, 1 - slot, True)

    consume(qi, lax.rem(qi, 2))
    for hh in range(heads):
        o_ref[:, hh * V_HEAD:(hh + 1) * V_HEAD] = (acc_scr[hh] / l_scr[hh]).T.astype(BF16)


def _attention(q, k, vt, peer_u, peer_v, tq, heads=2):
    h, s, _ = q.shape
    n_q = s // tq
    n_exp, d = peer_u.shape
    slab = n_exp // ((h // heads) * n_q)
    assert slab * (h // heads) * n_q == n_exp and slab % BF16_ROWS == 0
    table = pl.BlockSpec((slab, d), lambda g, i: (g * n_q + i, 0))
    return pl.pallas_call(
        functools.partial(_attn_kernel, tq=tq, heads=heads),
        grid=(h // heads, n_q),
        in_specs=[pl.BlockSpec((heads, tq, HEAD_PAD), lambda g, i: (g, i, 0)),
                  pl.BlockSpec((heads, s, HEAD_PAD), lambda g, i: (g, 0, 0)),
                  pl.BlockSpec((heads, V_HEAD, s), lambda g, i: (g, 0, 0)),
                  table, table],
        out_specs=[pl.BlockSpec((tq, heads * V_HEAD), lambda g, i: (i, g)), table, table],
        out_shape=[jax.ShapeDtypeStruct((s, h * V_HEAD), BF16),
                   jax.ShapeDtypeStruct((n_exp, d), BF16), jax.ShapeDtypeStruct((n_exp, d), BF16)],
        scratch_shapes=[pltpu.VMEM((heads, 2, tq, tq), F32), pltpu.VMEM((heads, 2, 1, tq), F32),
                        pltpu.VMEM((heads, 1, tq), F32), pltpu.VMEM((heads, 1, tq), F32),
                        pltpu.VMEM((heads, V_HEAD, tq), F32)],
        compiler_params=_cparams(2),
        name="attention",
    )(q, k, vt, peer_u, peer_v)


def _outproj_kernel(x_ref, conv_ref, attn_ref, w_ref, gate_ref, g_ref, sc_ref, sh_ref, x1_ref, h2_ref,
                    h2t_ref):
    mix = jnp.dot(conv_ref[...], w_ref[0:CONV_CH, :], preferred_element_type=F32)
    mix = mix + jnp.dot(attn_ref[...], w_ref[CONV_CH:, :], preferred_element_type=F32)
    x1 = x_ref[...] + gate_ref[...] * mix
    x1_ref[...] = x1
    ms = jnp.mean(x1 * x1, axis=-1, keepdims=True)
    h2 = x1 * lax.rsqrt(ms + EPS) * g_ref[...]
    h2 = h2 * (1.0 + sc_ref[...]) + sh_ref[...]
    h2_ref[...] = h2.astype(BF16)
    h2t_ref[...] = h2.T.astype(BF16)


def _out_proj(x, conv_out, attn_out, w_out, gate1, g2, scale2, shift2, tm):
    s, d = x.shape
    kdim = w_out.shape[0]
    row = lambda i: (i, 0)
    fixed = lambda i: (0, 0)
    return pl.pallas_call(
        _outproj_kernel,
        grid=(s // tm,),
        in_specs=[pl.BlockSpec((tm, d), row), pl.BlockSpec((tm, CONV_CH), row),
                  pl.BlockSpec((tm, kdim - CONV_CH), row),
                  pl.BlockSpec((kdim, d), fixed, pipeline_mode=pl.Buffered(1)),
                  pl.BlockSpec((1, d), fixed), pl.BlockSpec((1, d), fixed),
                  pl.BlockSpec((1, d), fixed), pl.BlockSpec((1, d), fixed)],
        out_specs=[pl.BlockSpec((tm, d), row), pl.BlockSpec((tm, d), row),
                   pl.BlockSpec((d, tm), lambda i: (0, i))],
        out_shape=[jax.ShapeDtypeStruct((s, d), F32), jax.ShapeDtypeStruct((s, d), BF16),
                   jax.ShapeDtypeStruct((d, s), BF16)],
        compiler_params=_cparams(1),
        name="out_proj",
    )(x, conv_out, attn_out, w_out, gate1, g2, scale2, shift2)


def _peer_scores_into(h2_ref, wq_ref, k1_ref, k2_ref, s1_ref, s2_ref):
    q = jnp.dot(h2_ref[...], wq_ref[...], preferred_element_type=F32).astype(BF16)
    for h in range(PEER_HEADS):
        base = h * 2 * PEER_HALF
        s1_ref[h] = _nt_dot(k1_ref[h], q[:, base:base + PEER_HALF])
        s2_ref[h] = _nt_dot(k2_ref[h], q[:, base + PEER_HALF:base + 2 * PEER_HALF])


_CAND_ROWS = 80
SELECT_LANES = 256


def _candidate_table():
    tbl = np.full((_CAND_ROWS,), -1, np.int32)
    for i in range(PEER_TOPK):
        tbl[i] = i * PEER_TOPK
    for j in range(1, 8):
        for i in range(8):
            if (i + 1) * (j + 1) <= PEER_TOPK:
                tbl[16 + 8 * (j - 1) + i] = i * PEER_TOPK + j
    for j in range(8, PEER_TOPK):
        tbl[72 + j - 8] = j
    return np.broadcast_to(tbl[:, None], (_CAND_ROWS, SELECT_LANES)).copy()


def _top16_rows(s, exact_ties):
    row = lax.broadcasted_iota(jnp.int32, s.shape, 0) if exact_ties else None
    rank = jnp.full(s.shape, float(PEER_TOPK), F32)
    vals = []
    for r in range(PEER_TOPK):
        m = jnp.max(s, axis=0, keepdims=True)
        if exact_ties:
            first = jnp.min(jnp.where(s == m, row, N_KEYS), axis=0, keepdims=True)
            sel = row == first
        else:
            sel = s == m
        rank = jnp.where(sel, float(r), rank)
        s = jnp.where(sel, NEG_INF, s)
        vals.append(m)
    return jnp.concatenate(vals, axis=0), rank


def _count_off(marked):
    cnt = jnp.sum(jnp.where(marked, 1.0, 0.0), axis=0, keepdims=True)
    return jnp.where(cnt != float(PEER_TOPK), 1.0, 0.0)


def _select_chunk(s1, s2, flat, exact_ties):
    valid = flat >= 0
    row8 = lax.broadcasted_iota(jnp.int32, (8, SELECT_LANES), 0)
    v1, rank1 = _top16_rows(s1, exact_ties)
    v2, rank2 = _top16_rows(s2, exact_ties)
    blocks = [v1 + v2[0:1]]
    for j in range(1, 8):
        blocks.append(v1[0:8] + v2[j:j + 1])
    blocks.append(v1[0:1] + v2[8:16])
    cand0 = jnp.where(valid, jnp.concatenate(blocks, axis=0), NEG_INF)
    cand = cand0
    chosen = jnp.zeros(cand.shape, F32)
    for _ in range(PEER_TOPK):
        m = jnp.max(cand, axis=0, keepdims=True)
        if exact_ties:
            first = jnp.min(jnp.where(cand == m, flat, PEER_TOPK * PEER_TOPK), axis=0, keepdims=True)
            sel = flat == first
        else:
            sel = cand == m
        chosen = jnp.where(sel, 1.0, chosen)
        cand = jnp.where(sel, NEG_INF, cand)
    n_lo = chosen[0:8]
    for j in range(1, 8):
        n_lo = n_lo + chosen[16 + 8 * (j - 1):16 + 8 * j]
    tail = jnp.sum(chosen[72:80], axis=0, keepdims=True)
    n_lo = n_lo + jnp.where(row8 == 0, tail, 0.0)
    n = jnp.concatenate([n_lo, chosen[8:16]], axis=0)
    top = v1[0:1] + v2[0:1]
    z = jnp.sum(jnp.where(chosen > 0.0, jnp.exp(cand0 - top), 0.0), axis=0, keepdims=True)
    n1 = jnp.zeros(s1.shape, F32)
    for r in range(PEER_TOPK):
        n1 = jnp.where(rank1 == float(r), n[r:r + 1], n1)
    off = (_count_off(rank1 < float(PEER_TOPK)) + _count_off(rank2 < float(PEER_TOPK))
           + _count_off(chosen > 0.0))
    outs = (n1 * RANK_SCALE, jnp.exp(s1 - v1[0:1]), (rank2 * RANK_SCALE).astype(BF16),
            (jnp.exp(s2 - v2[0:1]) / z).astype(BF16))
    return outs, off


def _peer_route_kernel(h2_ref, wq_ref, k1_ref, k2_ref, tbl_ref, n1_ref, a1_ref, r2_ref, a2_ref,
                       s1_scr, s2_scr, *, chunks):
    _peer_scores_into(h2_ref, wq_ref, k1_ref, k2_ref, s1_scr, s2_scr)
    flat = tbl_ref[...]
    out_refs = (n1_ref, a1_ref, r2_ref, a2_ref)

    def chunk(it, _):
        h = it // chunks
        c = it - h * chunks
        sl = pl.ds(pl.multiple_of(c * SELECT_LANES, SELECT_LANES), SELECT_LANES)
        outs, off = _select_chunk(s1_scr[h, :, sl], s2_scr[h, :, sl], flat, exact_ties=False)
        for ref, val in zip(out_refs, outs):
            ref[h, :, sl] = val

        @pl.when(jnp.sum(off) > 0.0)
        def _():
            outs_exact, _ = _select_chunk(s1_scr[h, :, sl], s2_scr[h, :, sl], flat, exact_ties=True)
            for ref, val in zip(out_refs, outs_exact):
                ref[h, :, sl] = val

        return 0

    lax.fori_loop(0, PEER_HEADS * chunks, chunk, 0)


def _peer_route(h2, wq, k1, k2, tm):
    t, d = h2.shape
    spec = pl.BlockSpec((PEER_HEADS, N_KEYS, tm), lambda i: (0, 0, i))
    keys = pl.BlockSpec((PEER_HEADS, N_KEYS, PEER_HALF), lambda i: (0, 0, 0))
    wide = jax.ShapeDtypeStruct((PEER_HEADS, N_KEYS, t), F32)
    narrow = jax.ShapeDtypeStruct((PEER_HEADS, N_KEYS, t), BF16)
    tbl = jnp.asarray(_candidate_table())
    return pl.pallas_call(
        functools.partial(_peer_route_kernel, chunks=tm // SELECT_LANES),
        grid=(t // tm,),
        in_specs=[pl.BlockSpec((tm, d), lambda i: (i, 0)),
                  pl.BlockSpec(wq.shape, lambda i: (0, 0), pipeline_mode=pl.Buffered(1)),
                  keys, keys, pl.BlockSpec((_CAND_ROWS, SELECT_LANES), lambda i: (0, 0))],
        out_specs=[spec] * 4,
        out_shape=[wide, wide, narrow, narrow],
        scratch_shapes=[pltpu.VMEM((PEER_HEADS, N_KEYS, tm), F32)] * 2,
        compiler_params=_cparams(1),
        name="peer_route",
    )(h2, wq, k1, k2, tbl)


def _gelu_tanh(x):
    return 0.5 * x * (1.0 + jnp.tanh(0.7978845608028654 * (x + 0.044715 * (x * x * x))))


def _zero_of(x):
    bits = lax.bitcast_convert_type(x, jnp.uint32)
    bits = lax.shift_right_logical(lax.shift_right_logical(bits, jnp.uint32(16)), jnp.uint32(16))
    return lax.bitcast_convert_type(bits, F32)


BF16_ROWS = 16


def _rows_bf16(row, n):
    tile = jnp.broadcast_to(row, (BF16_ROWS, row.shape[1])).astype(BF16)
    return jnp.tile(tile, (n // BF16_ROWS, 1))


def _peer_dense_kernel(h2t_ref, u_ref, v_ref, n1_ref, a1_ref, r2_ref, a2_ref, x1_ref, gate_ref, o_ref,
                       *a_scr, te, chunk):
    j = pl.program_id(1)

    @pl.when(j == 0)
    def _():
        o_ref[...] = jnp.zeros_like(o_ref)

    h2t = h2t_ref[...]
    zero = jnp.zeros((), BF16)
    per = chunk // N_KEYS
    n_chunks = te // chunk
    slot_w = lax.rem(j, 2)
    slot_r = lax.rem(j + 2, 2)

    def scores(q):
        a_scr[q][slot_w] = jnp.dot(u_ref[pl.ds(q * chunk, chunk), :], h2t, preferred_element_type=F32)

    scores(0)
    for q in range(n_chunks):
        if q + 1 < n_chunks:
            scores(q + 1)
        blocks = []
        for e in range(per):
            e1 = q * per + e
            pieces = []
            for c in range(h2t.shape[1] // LANES):
                cols = pl.ds(c * LANES, LANES)
                terms = []
                for h in range(PEER_HEADS):
                    n1 = _rows_bf16(n1_ref[h, pl.ds(e1, 1), cols], N_KEYS)
                    a1 = _rows_bf16(a1_ref[h, pl.ds(e1, 1), cols], N_KEYS)
                    keys = pl.ds(h * N_KEYS, N_KEYS)
                    chosen = jnp.maximum(n1 - r2_ref[keys, cols], zero)
                    terms.append(jnp.minimum(a1 * a2_ref[keys, cols], chosen))
                while len(terms) > 1:
                    terms = [terms[k] + terms[k + 1] for k in range(0, len(terms), 2)]
                a_t = a_scr[q][slot_r, pl.ds(e * N_KEYS, N_KEYS), cols]
                pieces.append(terms[0] * _gelu_tanh(a_t.astype(BF16)))
            blocks.append(jnp.concatenate(pieces, axis=1))
        act = jnp.concatenate(blocks, axis=0)
        if q + 1 < n_chunks:
            act = act + _zero_of(a_scr[q + 1][slot_r, 0:1, :]).astype(BF16)
        o_ref[...] += _tn_dot(act, v_ref[pl.ds(q * chunk, chunk), :])

    @pl.when(j == pl.num_programs(1) - 1)
    def _():
        o_ref[...] = x1_ref[...] + gate_ref[...] * o_ref[...]


def _peer_dense(h2t, u, v, n1, a1, r2, a2, x1, gate2, tm, te):
    d, t = h2t.shape
    n_j = u.shape[0] // te
    e1_per_tile = te // N_KEYS
    single = pl.Buffered(1)
    small = pl.BlockSpec((PEER_HEADS, e1_per_tile, tm), lambda i, j: (0, j, i))
    big = pl.BlockSpec((PEER_HEADS * N_KEYS, tm), lambda i, j: (0, i), pipeline_mode=single)
    r2 = r2.reshape(PEER_HEADS * N_KEYS, t)
    a2 = a2.reshape(PEER_HEADS * N_KEYS, t)
    chunk = 4 * N_KEYS
    return pl.pallas_call(
        functools.partial(_peer_dense_kernel, te=te, chunk=chunk),
        grid=(t // tm, n_j),
        in_specs=[pl.BlockSpec((d, tm), lambda i, j: (0, i), pipeline_mode=single),
                  pl.BlockSpec((te, d), lambda i, j: (j, 0)),
                  pl.BlockSpec((te, d), lambda i, j: (j, 0)),
                  small, small, big, big,
                  pl.BlockSpec((tm, d), lambda i, j: (i, 0), pipeline_mode=single),
                  pl.BlockSpec((1, d), lambda i, j: (0, 0))],
        out_specs=pl.BlockSpec((tm, d), lambda i, j: (i, 0)),
        out_shape=jax.ShapeDtypeStruct((t, d), F32),
        scratch_shapes=[pltpu.VMEM((2, chunk, tm), F32)] * (te // chunk),
        compiler_params=_cparams(2),
        name="peer_dense",
    )(h2t, u, v, n1, a1, r2, a2, x1, gate2)


def _pad_cols(w, width):
    return jnp.pad(w, ((0, 0), (0, width - w.shape[1])))


def _head_pad_cols(w, per_head):
    k = w.shape[0]
    w = w.reshape(k, MLA_HEADS, per_head)
    w = jnp.pad(w, ((0, 0), (0, 0), (0, HEAD_PAD - per_head)))
    return w.reshape(k, MLA_HEADS * HEAD_PAD)


def _layer(x, mod, positions_tables, p, tiles):
    d = x.shape[1]
    shift1, scale1, gate1, shift2, scale2, gate2 = [mod[:, i * d:(i + 1) * d] for i in range(6)]
    cos_t, sin_t = positions_tables

    n_main = 2 * CONV_CH + Q_LORA + KV_LORA
    w_main = p["w_in"][:, :n_main].astype(BF16)
    w_kr = _pad_cols(p["w_in"][:, n_main:], LANES).astype(BF16)
    a_conv, q_lat, kv_lat, k_rope = _in_proj(x, p["norm1_g"], scale1, shift1, w_main, w_kr, tiles["proj"])

    conv_out = _conv_module(a_conv, p["conv_dw_w"], p["conv_dw_b"], p["conv_ln_g"], p["conv_ln_b"],
                            tiles["conv"])

    wuq = _head_pad_cols(p["w_uq"], QK_HEAD).astype(BF16)
    gq = _pad_cols(p["q_norm_g"], HEAD_PAD)
    gk = _pad_cols(p["k_norm_g"], HEAD_PAD)
    q, k, vt = _mla_proj(q_lat, kv_lat, k_rope, cos_t, sin_t, p["q_a_norm_g"], p["kv_a_norm_g"],
                         wuq, p["w_ukv"].astype(BF16), gq, gk, tiles["proj"])
    attn_out, u_bf16, v_bf16 = _attention(q, k, vt, p["peer_u"], p["peer_v"], tiles["attn"])

    x1, h2, h2t = _out_proj(x, conv_out, attn_out, p["w_out"].astype(BF16), gate1, p["norm2_g"], scale2,
                            shift2, tiles["proj"])

    n1, a1, r2, a2 = _peer_route(h2, p["peer_wq"].astype(BF16), p["peer_k1"].astype(BF16),
                                 p["peer_k2"].astype(BF16), tiles["proj"])
    return _peer_dense(h2t, u_bf16, v_bf16, n1, a1, r2, a2, x1, gate2,
                       tiles["dense_tokens"], tiles["dense_experts"])


def _tiles(s):
    return {"proj": min(512, s), "conv": min(256, s), "attn": min(512, s),
            "dense_tokens": min(512, s), "dense_experts": 1024}


def kernel(x, c, positions, w_ada, b_ada, norm1_g, w_in, conv_dw_w, conv_dw_b, conv_ln_g, conv_ln_b,
           q_a_norm_g, w_uq, kv_a_norm_g, w_ukv, q_norm_g, k_norm_g, w_out, norm2_g, peer_wq, peer_k1,
           peer_k2, peer_u, peer_v):
    b, s, d = x.shape
    assert b == 1, "kernel is written for batch 1"
    depth = w_ada.shape[0]
    row = lambda a, l: a[l].reshape(1, -1)
    tables = _rope_tables(positions[0])
    xs = x[0]
    for l in range(depth):
        mod = _ada_mod(c, w_ada[l], b_ada[l])
        params = {
            "norm1_g": row(norm1_g, l), "w_in": w_in[l], "conv_dw_w": conv_dw_w[l],
            "conv_dw_b": row(conv_dw_b, l), "conv_ln_g": row(conv_ln_g, l), "conv_ln_b": row(conv_ln_b, l),
            "q_a_norm_g": row(q_a_norm_g, l), "w_uq": w_uq[l], "kv_a_norm_g": row(kv_a_norm_g, l),
            "w_ukv": w_ukv[l], "q_norm_g": row(q_norm_g, l), "k_norm_g": row(k_norm_g, l),
            "w_out": w_out[l], "norm2_g": row(norm2_g, l), "peer_wq": peer_wq[l],
            "peer_k1": peer_k1[l], "peer_k2": peer_k2[l], "peer_u": peer_u[l], "peer_v": peer_v[l],
        }
        xs = _layer(xs, mod, tables, params, _tiles(s))
    return xs[None]
```

```python
import functools

import numpy as np
import jax
import jax.numpy as jnp
from jax import lax
from jax.experimental import pallas as pl
from jax.experimental.pallas import tpu as pltpu

F32 = jnp.float32
BF16 = jnp.bfloat16

CHUNK = 64
EPS = 1e-6
CONV_CH = 1024
CONV_WIDTH = 31
MLA_HEADS = 8
QK_NOPE = 128
QK_ROPE = 64
QK_HEAD = QK_NOPE + QK_ROPE
V_HEAD = 128
Q_LORA = 768
KV_LORA = 512
ROPE_THETA = 10000.0
PEER_HEADS = 8
N_KEYS = 128
PEER_TOPK = 16
PEER_HALF = 128

LANES = 128
SUBLANES = 8
HEAD_PAD = 256
CONV_HALO = 32
VMEM_LIMIT = 56 * 1024 * 1024

RANK_SCALE = 1024.0
NEG_INF = float("-inf")
LOG2_E = 1.4426950408889634


def _cparams(n_axes, vmem=VMEM_LIMIT, flags=None):
    return pltpu.CompilerParams(dimension_semantics=("arbitrary",) * n_axes, vmem_limit_bytes=vmem,
                                flags=flags)


def _nt_dot(a, b):
    return lax.dot_general(a, b, (((1,), (1,)), ((), ())), preferred_element_type=F32)


def _tn_dot(a, b):
    return lax.dot_general(a, b, (((0,), (0,)), ((), ())), preferred_element_type=F32)


def _ada_kernel(c_ref, w_ref, b_ref, o_ref):
    c = c_ref[...]
    ca = (c * jax.nn.sigmoid(c)).astype(BF16)
    o_ref[...] = jnp.dot(ca, w_ref[...].astype(BF16), preferred_element_type=F32) + b_ref[...]


def _ada_mod(c, w_ada, b_ada):
    d = c.shape[1]
    n = w_ada.shape[1]
    tn = 1024
    c8 = jnp.broadcast_to(c, (8, d))
    out = pl.pallas_call(
        _ada_kernel,
        grid=(n // tn,),
        in_specs=[pl.BlockSpec((8, d), lambda j: (0, 0)),
                  pl.BlockSpec((d, tn), lambda j: (0, j)),
                  pl.BlockSpec((1, tn), lambda j: (0, j))],
        out_specs=pl.BlockSpec((8, tn), lambda j: (0, j)),
        out_shape=jax.ShapeDtypeStruct((8, n), F32),
        compiler_params=_cparams(1),
        name="ada_mod",
    )(c8, w_ada, b_ada.reshape(1, n))
    return out[0:1]


def _rope_table_kernel(pos_ref, invf_ref, cos_ref, sin_ref):
    ang = pos_ref[...] * invf_ref[...]
    cos_ref[...] = jnp.cos(ang)
    sin_ref[...] = jnp.sin(ang)


def _rope_tables(positions):
    s = positions.shape[0]
    half = QK_ROPE // 2
    rep = LANES // half
    inv_freq = ROPE_THETA ** (-jnp.arange(0, QK_ROPE, 2, dtype=F32) / QK_ROPE)
    pos_rep = jnp.repeat(positions.astype(F32), half).reshape(s // rep, LANES)
    invf = jnp.tile(inv_freq, rep).reshape(1, LANES)
    cos_d, sin_d = pl.pallas_call(
        _rope_table_kernel,
        out_shape=(jax.ShapeDtypeStruct((s // rep, LANES), F32),) * 2,
        name="rope_table",
    )(pos_rep, invf)
    cos = cos_d.reshape(s, half)
    sin = sin_d.reshape(s, half)
    zeros = jnp.zeros((s, LANES - QK_ROPE), F32)
    return (jnp.concatenate([cos, cos, zeros], axis=1),
            jnp.concatenate([-sin, sin, zeros], axis=1))


def _inproj_kernel(x_ref, g_ref, sc_ref, sh_ref, w_ref, wkr_ref, conv_ref, ql_ref, kvl_ref, kr_ref):
    x = x_ref[...]
    ms = jnp.mean(x * x, axis=-1, keepdims=True)
    h = x * lax.rsqrt(ms + EPS) * g_ref[...]
    h = (h * (1.0 + sc_ref[...]) + sh_ref[...]).astype(BF16)
    a = jnp.dot(h, w_ref[...], preferred_element_type=F32)
    o1 = 2 * CONV_CH
    o2 = o1 + Q_LORA
    conv_ref[...] = a[:, :o1]
    ql_ref[...] = a[:, o1:o2]
    kvl_ref[...] = a[:, o2:]
    kr_ref[...] = jnp.dot(h, wkr_ref[...], preferred_element_type=F32)


def _in_proj(x, g, scale, shift, w_main, w_kr, tm):
    s, d = x.shape
    n = w_main.shape[1]
    row = lambda i: (i, 0)
    fixed = lambda i: (0, 0)
    return pl.pallas_call(
        _inproj_kernel,
        grid=(s // tm,),
        in_specs=[pl.BlockSpec((tm, d), row),
                  pl.BlockSpec((1, d), fixed), pl.BlockSpec((1, d), fixed), pl.BlockSpec((1, d), fixed),
                  pl.BlockSpec((d, n), fixed, pipeline_mode=pl.Buffered(1)),
                  pl.BlockSpec((d, LANES), fixed)],
        out_specs=[pl.BlockSpec((tm, 2 * CONV_CH), row), pl.BlockSpec((tm, Q_LORA), row),
                   pl.BlockSpec((tm, KV_LORA), row), pl.BlockSpec((tm, LANES), row)],
        out_shape=[jax.ShapeDtypeStruct((s, 2 * CONV_CH), F32), jax.ShapeDtypeStruct((s, Q_LORA), F32),
                   jax.ShapeDtypeStruct((s, KV_LORA), F32), jax.ShapeDtypeStruct((s, LANES), F32)],
        compiler_params=_cparams(1),
        name="in_proj",
    )(x, g, scale, shift, w_main, w_kr)


def _conv_kernel(cur_ref, halo_ref, w_ref, b_ref, g_ref, beta_ref, o_ref, hbuf, *, tm, rc):
    i = pl.program_id(0)
    cur = cur_ref[...]
    hbuf[pl.ds(CONV_HALO, tm), :] = cur[:, :CONV_CH] * jax.nn.sigmoid(cur[:, CONV_CH:])
    hal = halo_ref[...]
    hg = hal[:, :CONV_CH] * jax.nn.sigmoid(hal[:, CONV_CH:])
    hbuf[pl.ds(0, CONV_HALO), :] = jnp.where(i > 0, hg, 0.0)
    first = CONV_HALO - (CONV_WIDTH - 1)
    span = rc + CONV_HALO

    def row_chunk(r, _):
        r0 = pl.multiple_of(r * rc, rc)
        accs = []
        for c in range(CONV_CH // LANES):
            cols = pl.ds(c * LANES, LANES)
            x = hbuf[pl.ds(r0, span), cols]
            acc = jnp.zeros((rc, LANES), F32)
            for res in range(SUBLANES):
                taps = [k for k in range(CONV_WIDTH) if (first + k) % SUBLANES == res]
                xs = x if res == 0 else pltpu.roll(x, span - res, 0)
                for k in taps:
                    a = (first + k) // SUBLANES
                    acc = acc + w_ref[pl.ds(k, 1), cols] * xs[SUBLANES * a:SUBLANES * a + rc]
            accs.append(acc)
        acc = jnp.concatenate(accs, axis=1) + b_ref[...]
        mu = jnp.mean(acc, axis=-1, keepdims=True)
        cen = acc - mu
        var = jnp.mean(cen * cen, axis=-1, keepdims=True)
        y = cen * lax.rsqrt(var + EPS) * g_ref[...] + beta_ref[...]
        o_ref[pl.ds(r0, rc), :] = (y * jax.nn.sigmoid(y)).astype(BF16)
        return 0

    lax.fori_loop(0, tm // rc, row_chunk, 0)


def _conv_module(a_conv, dw_w, dw_b, ln_g, ln_b, tm):
    s = a_conv.shape[0]
    hb = tm // CONV_HALO
    fixed = lambda i: (0, 0)
    return pl.pallas_call(
        functools.partial(_conv_kernel, tm=tm, rc=32),
        grid=(s // tm,),
        in_specs=[pl.BlockSpec((tm, 2 * CONV_CH), lambda i: (i, 0)),
                  pl.BlockSpec((CONV_HALO, 2 * CONV_CH), lambda i: (jnp.maximum(i * hb - 1, 0), 0)),
                  pl.BlockSpec((CONV_WIDTH, CONV_CH), fixed),
                  pl.BlockSpec((1, CONV_CH), fixed), pl.BlockSpec((1, CONV_CH), fixed),
                  pl.BlockSpec((1, CONV_CH), fixed)],
        out_specs=pl.BlockSpec((tm, CONV_CH), lambda i: (i, 0)),
        out_shape=jax.ShapeDtypeStruct((s, CONV_CH), BF16),
        scratch_shapes=[pltpu.VMEM((tm + CONV_HALO, CONV_CH), F32)],
        compiler_params=_cparams(1),
        name="conv_module",
    )(a_conv, a_conv, dw_w, dw_b, ln_g, ln_b)


def _mla_proj_kernel(ql_ref, kvl_ref, kr_ref, cos_ref, sin_ref, gqa_ref, gkva_ref, wuq_ref, wukv_ref,
                     gq_ref, gk_ref, q_out, k_out, vt_out):
    def rms(v, g):
        return v * lax.rsqrt(jnp.mean(v * v, axis=-1, keepdims=True) + EPS) * g

    q = jnp.dot(rms(ql_ref[...], gqa_ref[...]).astype(BF16), wuq_ref[...], preferred_element_type=F32)
    kv = jnp.dot(rms(kvl_ref[...], gkva_ref[...]).astype(BF16), wukv_ref[...], preferred_element_type=F32)
    kr = kr_ref[...]
    cos = cos_ref[...]
    sin = sin_ref[...]
    lane = lax.broadcasted_iota(jnp.int32, kr.shape, 1)
    half = QK_ROPE // 2

    def rope(v):
        rot = jnp.where(lane < half, pltpu.roll(v, LANES - half, 1), pltpu.roll(v, half, 1))
        return v * cos + rot * sin

    gq = gq_ref[...]
    gk = gk_ref[...]
    scale = QK_HEAD ** -0.5 * LOG2_E
    ss_kr = jnp.sum(kr * kr, axis=-1, keepdims=True)
    for h in range(MLA_HEADS):
        qh = q[:, h * HEAD_PAD:(h + 1) * HEAD_PAD]
        rq = lax.rsqrt(jnp.sum(qh * qh, axis=-1, keepdims=True) / QK_HEAD + EPS)
        qh = qh * rq * gq
        q_out[h, :, 0:QK_NOPE] = (qh[:, :QK_NOPE] * scale).astype(BF16)
        q_out[h, :, QK_NOPE:HEAD_PAD] = (rope(qh[:, QK_NOPE:]) * scale).astype(BF16)
        kn = kv[:, h * HEAD_PAD:h * HEAD_PAD + QK_NOPE]
        vh = kv[:, h * HEAD_PAD + QK_NOPE:(h + 1) * HEAD_PAD]
        rk = lax.rsqrt((jnp.sum(kn * kn, axis=-1, keepdims=True) + ss_kr) / QK_HEAD + EPS)
        k_out[h, :, 0:QK_NOPE] = (kn * rk * gk[:, :QK_NOPE]).astype(BF16)
        k_out[h, :, QK_NOPE:HEAD_PAD] = rope(kr * rk * gk[:, QK_NOPE:]).astype(BF16)
        vt_out[h] = vh.T.astype(BF16)


def _mla_proj(q_lat, kv_lat, k_rope, cos_t, sin_t, gqa, gkva, wuq_pad, wukv, gq_pad, gk_pad, tm):
    s = q_lat.shape[0]
    row = lambda i: (i, 0)
    fixed = lambda i: (0, 0)
    hq = MLA_HEADS * HEAD_PAD
    return pl.pallas_call(
        _mla_proj_kernel,
        grid=(s // tm,),
        in_specs=[pl.BlockSpec((tm, Q_LORA), row), pl.BlockSpec((tm, KV_LORA), row),
                  pl.BlockSpec((tm, LANES), row), pl.BlockSpec((tm, LANES), row), pl.BlockSpec((tm, LANES), row),
                  pl.BlockSpec((1, Q_LORA), fixed), pl.BlockSpec((1, KV_LORA), fixed),
                  pl.BlockSpec((Q_LORA, hq), fixed), pl.BlockSpec((KV_LORA, hq), fixed),
                  pl.BlockSpec((1, HEAD_PAD), fixed), pl.BlockSpec((1, HEAD_PAD), fixed)],
        out_specs=[pl.BlockSpec((MLA_HEADS, tm, HEAD_PAD), lambda i: (0, i, 0)),
                   pl.BlockSpec((MLA_HEADS, tm, HEAD_PAD), lambda i: (0, i, 0)),
                   pl.BlockSpec((MLA_HEADS, V_HEAD, tm), lambda i: (0, 0, i))],
        out_shape=[jax.ShapeDtypeStruct((MLA_HEADS, s, HEAD_PAD), BF16),
                   jax.ShapeDtypeStruct((MLA_HEADS, s, HEAD_PAD), BF16),
                   jax.ShapeDtypeStruct((MLA_HEADS, V_HEAD, s), BF16)],
        compiler_params=_cparams(1),
        name="mla_proj",
    )(q_lat, kv_lat, k_rope, cos_t, sin_t, gqa, gkva, wuq_pad, wukv, gq_pad, gk_pad)


def _attn_kernel(q_ref, k_ref, vt_ref, u_ref, v_ref, o_ref, ub_ref, vb_ref, s_scr, mt_scr, m_scr, l_scr,
                 acc_scr, *, tq, heads):
    qi = pl.program_id(1)
    ub_ref[...] = u_ref[...].astype(BF16)
    vb_ref[...] = v_ref[...].astype(BF16)

    def produce(kt, slot, masked):
        k0 = pl.multiple_of(kt * tq, tq)
        for hh in range(heads):
            s = _nt_dot(k_ref[hh, pl.ds(k0, tq), :], q_ref[hh])
            if masked:
                kc = (k0 + lax.broadcasted_iota(jnp.int32, s.shape, 0)) // CHUNK
                qc = (qi * tq + lax.broadcasted_iota(jnp.int32, s.shape, 1)) // CHUNK
                s = jnp.where(kc <= qc, s, NEG_INF)
            s_scr[hh, slot] = s
            mt_scr[hh, slot] = jnp.max(s, axis=0, keepdims=True)

    def consume(kt, slot):
        k0 = pl.multiple_of(kt * tq, tq)
        for hh in range(heads):
            m = m_scr[hh]
            m_new = jnp.maximum(m, mt_scr[hh, slot])
            alpha = jnp.exp2(m - m_new)
            p = jnp.exp2(s_scr[hh, slot] - m_new)
            l_scr[hh] = alpha * l_scr[hh] + jnp.sum(p, axis=0, keepdims=True)
            pv = jnp.dot(vt_ref[hh, :, pl.ds(k0, tq)], p.astype(BF16), preferred_element_type=F32)
            acc_scr[hh] = alpha * acc_scr[hh] + pv
            m_scr[hh] = m_new

    m_scr[...] = jnp.full(m_scr.shape, NEG_INF, F32)
    l_scr[...] = jnp.zeros_like(l_scr)
    acc_scr[...] = jnp.zeros_like(acc_scr)

    @pl.when(qi == 0)
    def _():
        produce(0, 0, True)

    @pl.when(qi > 0)
    def _():
        produce(0, 0, False)

    def body(kt, _):
        slot = lax.rem(kt, 2)
        consume(kt, slot)
        produce(kt + 1, 1 - slot, False)
        return 0

    lax.fori_loop(0, jnp.maximum(qi - 1, 0), body, 0)

    @pl.when(qi > 0)
    def _():
        slot = lax.rem(qi - 1, 2)
        consume(qi - 1, slot)
        produce(qi, 1 - slot, True)

    consume(qi, lax.rem(qi, 2))
    for hh in range(heads):
        o_ref[:, hh * V_HEAD:(hh + 1) * V_HEAD] = (acc_scr[hh] / l_scr[hh]).T.astype(BF16)


def _attention(q, k, vt, peer_u, peer_v, tq, heads=2):
    h, s, _ = q.shape
    n_q = s // tq
    n_exp, d = peer_u.shape
    slab = n_exp // ((h // heads) * n_q)
    assert slab * (h // heads) * n_q == n_exp and slab % BF16_ROWS == 0
    table = pl.BlockSpec((slab, d), lambda g, i: (g * n_q + i, 0))
    return pl.pallas_call(
        functools.partial(_attn_kernel, tq=tq, heads=heads),
        grid=(h // heads, n_q),
        in_specs=[pl.BlockSpec((heads, tq, HEAD_PAD), lambda g, i: (g, i, 0)),
                  pl.BlockSpec((heads, s, HEAD_PAD), lambda g, i: (g, 0, 0)),
                  pl.BlockSpec((heads, V_HEAD, s), lambda g, i: (g, 0, 0)),
                  table, table],
        out_specs=[pl.BlockSpec((tq, heads * V_HEAD), lambda g, i: (i, g)), table, table],
        out_shape=[jax.ShapeDtypeStruct((s, h * V_HEAD), BF16),
                   jax.ShapeDtypeStruct((n_exp, d), BF16), jax.ShapeDtypeStruct((n_exp, d), BF16)],
        scratch_shapes=[pltpu.VMEM((heads, 2, tq, tq), F32), pltpu.VMEM((heads, 2, 1, tq), F32),
                        pltpu.VMEM((heads, 1, tq), F32), pltpu.VMEM((heads, 1, tq), F32),
                        pltpu.VMEM((heads, V_HEAD, tq), F32)],
        compiler_params=_cparams(2),
        name="attention",
    )(q, k, vt, peer_u, peer_v)


def _outproj_kernel(x_ref, conv_ref, attn_ref, w_ref, gate_ref, g_ref, sc_ref, sh_ref, x1_ref, h2_ref,
                    h2t_ref):
    mix = jnp.dot(conv_ref[...], w_ref[0:CONV_CH, :], preferred_element_type=F32)
    mix = mix + jnp.dot(attn_ref[...], w_ref[CONV_CH:, :], preferred_element_type=F32)
    x1 = x_ref[...] + gate_ref[...] * mix
    x1_ref[...] = x1
    ms = jnp.mean(x1 * x1, axis=-1, keepdims=True)
    h2 = x1 * lax.rsqrt(ms + EPS) * g_ref[...]
    h2 = h2 * (1.0 + sc_ref[...]) + sh_ref[...]
    h2_ref[...] = h2.astype(BF16)
    h2t_ref[...] = h2.T.astype(BF16)


def _out_proj(x, conv_out, attn_out, w_out, gate1, g2, scale2, shift2, tm):
    s, d = x.shape
    kdim = w_out.shape[0]
    row = lambda i: (i, 0)
    fixed = lambda i: (0, 0)
    return pl.pallas_call(
        _outproj_kernel,
        grid=(s // tm,),
        in_specs=[pl.BlockSpec((tm, d), row), pl.BlockSpec((tm, CONV_CH), row),
                  pl.BlockSpec((tm, kdim - CONV_CH), row),
                  pl.BlockSpec((kdim, d), fixed, pipeline_mode=pl.Buffered(1)),
                  pl.BlockSpec((1, d), fixed), pl.BlockSpec((1, d), fixed),
                  pl.BlockSpec((1, d), fixed), pl.BlockSpec((1, d), fixed)],
        out_specs=[pl.BlockSpec((tm, d), row), pl.BlockSpec((tm, d), row),
                   pl.BlockSpec((d, tm), lambda i: (0, i))],
        out_shape=[jax.ShapeDtypeStruct((s, d), F32), jax.ShapeDtypeStruct((s, d), BF16),
                   jax.ShapeDtypeStruct((d, s), BF16)],
        compiler_params=_cparams(1),
        name="out_proj",
    )(x, conv_out, attn_out, w_out, gate1, g2, scale2, shift2)


def _peer_scores_into(h2_ref, wq_ref, k1_ref, k2_ref, s1_ref, s2_ref):
    q = jnp.dot(h2_ref[...], wq_ref[...], preferred_element_type=F32).astype(BF16)
    for h in range(PEER_HEADS):
        base = h * 2 * PEER_HALF
        s1_ref[h] = _nt_dot(k1_ref[h], q[:, base:base + PEER_HALF])
        s2_ref[h] = _nt_dot(k2_ref[h], q[:, base + PEER_HALF:base + 2 * PEER_HALF])


_CAND_ROWS = 80
SELECT_LANES = 256


def _candidate_table():
    tbl = np.full((_CAND_ROWS,), -1, np.int32)
    for i in range(PEER_TOPK):
        tbl[i] = i * PEER_TOPK
    for j in range(1, 8):
        for i in range(8):
            if (i + 1) * (j + 1) <= PEER_TOPK:
                tbl[16 + 8 * (j - 1) + i] = i * PEER_TOPK + j
    for j in range(8, PEER_TOPK):
        tbl[72 + j - 8] = j
    return np.broadcast_to(tbl[:, None], (_CAND_ROWS, SELECT_LANES)).copy()


def _top16_rows(s, exact_ties):
    row = lax.broadcasted_iota(jnp.int32, s.shape, 0) if exact_ties else None
    rank = jnp.full(s.shape, float(PEER_TOPK), F32)
    vals = []
    for r in range(PEER_TOPK):
        m = jnp.max(s, axis=0, keepdims=True)
        if exact_ties:
            first = jnp.min(jnp.where(s == m, row, N_KEYS), axis=0, keepdims=True)
            sel = row == first
        else:
            sel = s == m
        rank = jnp.where(sel, float(r), rank)
        s = jnp.where(sel, NEG_INF, s)
        vals.append(m)
    return jnp.concatenate(vals, axis=0), rank


def _count_off(marked):
    cnt = jnp.sum(jnp.where(marked, 1.0, 0.0), axis=0, keepdims=True)
    return jnp.where(cnt != float(PEER_TOPK), 1.0, 0.0)


def _select_chunk(s1, s2, flat, exact_ties):
    valid = flat >= 0
    row8 = lax.broadcasted_iota(jnp.int32, (8, SELECT_LANES), 0)
    v1, rank1 = _top16_rows(s1, exact_ties)
    v2, rank2 = _top16_rows(s2, exact_ties)
    blocks = [v1 + v2[0:1]]
    for j in range(1, 8):
        blocks.append(v1[0:8] + v2[j:j + 1])
    blocks.append(v1[0:1] + v2[8:16])
    cand0 = jnp.where(valid, jnp.concatenate(blocks, axis=0), NEG_INF)
    cand = cand0
    for _ in range(PEER_TOPK):
        m = jnp.max(cand, axis=0, keepdims=True)
        if exact_ties:
            first = jnp.min(jnp.where(cand == m, flat, PEER_TOPK * PEER_TOPK), axis=0, keepdims=True)
            sel = flat == first
        else:
            sel = cand == m
        cand = jnp.where(sel, NEG_INF, cand)
    chosen = jnp.where(cand == NEG_INF, jnp.where(valid, 1.0, 0.0), 0.0)
    n_lo = chosen[0:8]
    for j in range(1, 8):
        n_lo = n_lo + chosen[16 + 8 * (j - 1):16 + 8 * j]
    tail = jnp.sum(chosen[72:80], axis=0, keepdims=True)
    n_lo = n_lo + jnp.where(row8 == 0, tail, 0.0)
    n = jnp.concatenate([n_lo, chosen[8:16]], axis=0)
    top = v1[0:1] + v2[0:1]
    z = jnp.sum(jnp.where(chosen > 0.0, jnp.exp(cand0 - top), 0.0), axis=0, keepdims=True)
    n1 = jnp.zeros(s1.shape, F32)
    for r in range(PEER_TOPK):
        n1 = jnp.where(rank1 == float(r), n[r:r + 1], n1)
    off = (_count_off(rank1 < float(PEER_TOPK)) + _count_off(rank2 < float(PEER_TOPK))
           + _count_off(chosen > 0.0))
    outs = (n1 * RANK_SCALE, jnp.exp(s1 - v1[0:1]), (rank2 * RANK_SCALE).astype(BF16),
            (jnp.exp(s2 - v2[0:1]) / z).astype(BF16))
    return outs, off


def _peer_route_kernel(h2_ref, wq_ref, k1_ref, k2_ref, tbl_ref, n1_ref, a1_ref, r2_ref, a2_ref,
                       s1_scr, s2_scr, *, chunks):
    _peer_scores_into(h2_ref, wq_ref, k1_ref, k2_ref, s1_scr, s2_scr)
    flat = tbl_ref[...]
    out_refs = (n1_ref, a1_ref, r2_ref, a2_ref)

    def chunk(it, _):
        h = it // chunks
        c = it - h * chunks
        sl = pl.ds(pl.multiple_of(c * SELECT_LANES, SELECT_LANES), SELECT_LANES)
        outs, off = _select_chunk(s1_scr[h, :, sl], s2_scr[h, :, sl], flat, exact_ties=False)
        for ref, val in zip(out_refs, outs):
            ref[h, :, sl] = val

        @pl.when(jnp.sum(off) > 0.0)
        def _():
            outs_exact, _ = _select_chunk(s1_scr[h, :, sl], s2_scr[h, :, sl], flat, exact_ties=True)
            for ref, val in zip(out_refs, outs_exact):
                ref[h, :, sl] = val

        return 0

    lax.fori_loop(0, PEER_HEADS * chunks, chunk, 0)


def _peer_route(h2, wq, k1, k2, tm):
    t, d = h2.shape
    spec = pl.BlockSpec((PEER_HEADS, N_KEYS, tm), lambda i: (0, 0, i))
    keys = pl.BlockSpec((PEER_HEADS, N_KEYS, PEER_HALF), lambda i: (0, 0, 0))
    wide = jax.ShapeDtypeStruct((PEER_HEADS, N_KEYS, t), F32)
    narrow = jax.ShapeDtypeStruct((PEER_HEADS, N_KEYS, t), BF16)
    tbl = jnp.asarray(_candidate_table())
    return pl.pallas_call(
        functools.partial(_peer_route_kernel, chunks=tm // SELECT_LANES),
        grid=(t // tm,),
        in_specs=[pl.BlockSpec((tm, d), lambda i: (i, 0)),
                  pl.BlockSpec(wq.shape, lambda i: (0, 0), pipeline_mode=pl.Buffered(1)),
                  keys, keys, pl.BlockSpec((_CAND_ROWS, SELECT_LANES), lambda i: (0, 0))],
        out_specs=[spec] * 4,
        out_shape=[wide, wide, narrow, narrow],
        scratch_shapes=[pltpu.VMEM((PEER_HEADS, N_KEYS, tm), F32)] * 2,
        compiler_params=_cparams(1),
        name="peer_route",
    )(h2, wq, k1, k2, tbl)


def _gelu_tanh(x):
    return 0.5 * x * (1.0 + jnp.tanh(0.7978845608028654 * (x + 0.044715 * (x * x * x))))


def _zero_of(x):
    bits = lax.bitcast_convert_type(x, jnp.uint32)
    bits = lax.shift_right_logical(lax.shift_right_logical(bits, jnp.uint32(16)), jnp.uint32(16))
    return lax.bitcast_convert_type(bits, F32)


BF16_ROWS = 16


def _rows_bf16(row, n):
    tile = jnp.broadcast_to(row, (BF16_ROWS, row.shape[1])).astype(BF16)
    return jnp.tile(tile, (n // BF16_ROWS, 1))


def _peer_dense_kernel(h2t_ref, u_ref, v_ref, n1_ref, a1_ref, r2_ref, a2_ref, x1_ref, gate_ref, o_ref,
                       *a_scr, sizes):
    j = pl.program_id(1)

    @pl.when(j == 0)
    def _():
        o_ref[...] = jnp.zeros_like(o_ref)

    h2t = h2t_ref[...]
    zero = jnp.zeros((), BF16)
    n_chunks = len(sizes)
    starts = [sum(sizes[:q]) for q in range(n_chunks)]
    slot_w = lax.rem(j, 2)
    slot_r = lax.rem(j + 2, 2)

    def scores(q):
        a_scr[q][slot_w] = jnp.dot(u_ref[pl.ds(starts[q], sizes[q]), :], h2t, preferred_element_type=F32)

    scores(0)
    for q in range(n_chunks):
        chunk = sizes[q]
        per = chunk // N_KEYS
        if q + 1 < n_chunks:
            scores(q + 1)
        blocks = []
        for e in range(per):
            e1 = starts[q] // N_KEYS + e
            pieces = []
            for c in range(h2t.shape[1] // LANES):
                cols = pl.ds(c * LANES, LANES)
                terms = []
                for h in range(PEER_HEADS):
                    n1 = _rows_bf16(n1_ref[h, pl.ds(e1, 1), cols], N_KEYS)
                    a1 = _rows_bf16(a1_ref[h, pl.ds(e1, 1), cols], N_KEYS)
                    keys = pl.ds(h * N_KEYS, N_KEYS)
                    chosen = jnp.maximum(n1 - r2_ref[keys, cols], zero)
                    terms.append(jnp.minimum(a1 * a2_ref[keys, cols], chosen))
                while len(terms) > 1:
                    terms = [terms[k] + terms[k + 1] for k in range(0, len(terms), 2)]
                a_t = a_scr[q][slot_r, pl.ds(e * N_KEYS, N_KEYS), cols]
                pieces.append(terms[0] * _gelu_tanh(a_t.astype(BF16)))
            blocks.append(jnp.concatenate(pieces, axis=1))
        act = jnp.concatenate(blocks, axis=0)
        if q + 1 < n_chunks:
            act = act + _zero_of(a_scr[q + 1][slot_r, 0:1, :]).astype(BF16)
        o_ref[...] += _tn_dot(act, v_ref[pl.ds(starts[q], chunk), :])

    @pl.when(j == pl.num_programs(1) - 1)
    def _():
        o_ref[...] = x1_ref[...] + gate_ref[...] * o_ref[...]


def _peer_dense(h2t, u, v, n1, a1, r2, a2, x1, gate2, tm, te):
    d, t = h2t.shape
    n_j = u.shape[0] // te
    e1_per_tile = te // N_KEYS
    small = pl.BlockSpec((PEER_HEADS, e1_per_tile, tm), lambda i, j: (0, j, i))
    big = pl.BlockSpec((PEER_HEADS * N_KEYS, tm), lambda i, j: (0, i))
    r2 = r2.reshape(PEER_HEADS * N_KEYS, t)
    a2 = a2.reshape(PEER_HEADS * N_KEYS, t)
    sizes = (te // 2, te // 2)
    return pl.pallas_call(
        functools.partial(_peer_dense_kernel, sizes=sizes),
        grid=(t // tm, n_j),
        in_specs=[pl.BlockSpec((d, tm), lambda i, j: (0, i)),
                  pl.BlockSpec((te, d), lambda i, j: (j, 0)),
                  pl.BlockSpec((te, d), lambda i, j: (j, 0)),
                  small, small, big, big,
                  pl.BlockSpec((tm, d), lambda i, j: (i, 0)),
                  pl.BlockSpec((1, d), lambda i, j: (0, 0))],
        out_specs=pl.BlockSpec((tm, d), lambda i, j: (i, 0)),
        out_shape=jax.ShapeDtypeStruct((t, d), F32),
        scratch_shapes=[pltpu.VMEM((2, size, tm), F32) for size in sizes],
        compiler_params=_cparams(2),
        name="peer_dense",
    )(h2t, u, v, n1, a1, r2, a2, x1, gate2)


def _pad_cols(w, width):
    return jnp.pad(w, ((0, 0), (0, width - w.shape[1])))


def _head_pad_cols(w, per_head):
    k = w.shape[0]
    w = w.reshape(k, MLA_HEADS, per_head)
    w = jnp.pad(w, ((0, 0), (0, 0), (0, HEAD_PAD - per_head)))
    return w.reshape(k, MLA_HEADS * HEAD_PAD)


def _layer(x, mod, positions_tables, p, tiles):
    d = x.shape[1]
    shift1, scale1, gate1, shift2, scale2, gate2 = [mod[:, i * d:(i + 1) * d] for i in range(6)]
    cos_t, sin_t = positions_tables

    n_main = 2 * CONV_CH + Q_LORA + KV_LORA
    w_main = p["w_in"][:, :n_main].astype(BF16)
    w_kr = _pad_cols(p["w_in"][:, n_main:], LANES).astype(BF16)
    a_conv, q_lat, kv_lat, k_rope = _in_proj(x, p["norm1_g"], scale1, shift1, w_main, w_kr, tiles["proj"])

    conv_out = _conv_module(a_conv, p["conv_dw_w"], p["conv_dw_b"], p["conv_ln_g"], p["conv_ln_b"],
                            tiles["conv"])

    wuq = _head_pad_cols(p["w_uq"], QK_HEAD).astype(BF16)
    gq = _pad_cols(p["q_norm_g"], HEAD_PAD)
    gk = _pad_cols(p["k_norm_g"], HEAD_PAD)
    q, k, vt = _mla_proj(q_lat, kv_lat, k_rope, cos_t, sin_t, p["q_a_norm_g"], p["kv_a_norm_g"],
                         wuq, p["w_ukv"].astype(BF16), gq, gk, tiles["proj"])
    attn_out, u_bf16, v_bf16 = _attention(q, k, vt, p["peer_u"], p["peer_v"], tiles["attn"])

    x1, h2, h2t = _out_proj(x, conv_out, attn_out, p["w_out"].astype(BF16), gate1, p["norm2_g"], scale2,
                            shift2, tiles["proj"])

    n1, a1, r2, a2 = _peer_route(h2, p["peer_wq"].astype(BF16), p["peer_k1"].astype(BF16),
                                 p["peer_k2"].astype(BF16), tiles["proj"])
    return _peer_dense(h2t, u_bf16, v_bf16, n1, a1, r2, a2, x1, gate2,
                       tiles["dense_tokens"], tiles["dense_experts"])


def _tiles(s):
    return {"proj": min(512, s), "conv": min(256, s), "attn": min(512, s),
            "dense_tokens": min(512, s), "dense_experts": 1024}


def kernel(x, c, positions, w_ada, b_ada, norm1_g, w_in, conv_dw_w, conv_dw_b, conv_ln_g, conv_ln_b,
           q_a_norm_g, w_uq, kv_a_norm_g, w_ukv, q_norm_g, k_norm_g, w_out, norm2_g, peer_wq, peer_k1,
           peer_k2, peer_u, peer_v):
    b, s, d = x.shape
    assert b == 1, "kernel is written for batch 1"
    depth = w_ada.shape[0]
    row = lambda a, l: a[l].reshape(1, -1)
    tables = _rope_tables(positions[0])
    xs = x[0]
    for l in range(depth):
        mod = _ada_mod(c, w_ada[l], b_ada[l])
        params = {
            "norm1_g": row(norm1_g, l), "w_in": w_in[l], "conv_dw_w": conv_dw_w[l],
            "conv_dw_b": row(conv_dw_b, l), "conv_ln_g": row(conv_ln_g, l), "conv_ln_b": row(conv_ln_b, l),
            "q_a_norm_g": row(q_a_norm_g, l), "w_uq": w_uq[l], "kv_a_norm_g": row(kv_a_norm_g, l),
            "w_ukv": w_ukv[l], "q_norm_g": row(q_norm_g, l), "k_norm_g": row(k_norm_g, l),
            "w_out": w_out[l], "norm2_g": row(norm2_g, l), "peer_wq": peer_wq[l],
            "peer_k1": peer_k1[l], "peer_k2": peer_k2[l], "peer_u": peer_u[l], "peer_v": peer_v[l],
        }
        xs = _layer(xs, mod, tables, params, _tiles(s))
    return xs[None]
```

```python
import functools

import numpy as np
import jax
import jax.numpy as jnp
from jax import lax
from jax.experimental import pallas as pl
from jax.experimental.pallas import tpu as pltpu

F32 = jnp.float32
BF16 = jnp.bfloat16

CHUNK = 64
EPS = 1e-6
CONV_CH = 1024
CONV_WIDTH = 31
MLA_HEADS = 8
QK_NOPE = 128
QK_ROPE = 64
QK_HEAD = QK_NOPE + QK_ROPE
V_HEAD = 128
Q_LORA = 768
KV_LORA = 512
ROPE_THETA = 10000.0
PEER_HEADS = 8
N_KEYS = 128
PEER_TOPK = 16
PEER_HALF = 128

LANES = 128
SUBLANES = 8
HEAD_PAD = 256
CONV_HALO = 32
VMEM_LIMIT = 56 * 1024 * 1024

RANK_SCALE = 1024.0
NEG_INF = float("-inf")
LOG2_E = 1.4426950408889634


def _cparams(n_axes, vmem=VMEM_LIMIT, flags=None):
    return pltpu.CompilerParams(dimension_semantics=("arbitrary",) * n_axes, vmem_limit_bytes=vmem,
                                flags=flags)


def _nt_dot(a, b):
    return lax.dot_general(a, b, (((1,), (1,)), ((), ())), preferred_element_type=F32)


def _tn_dot(a, b):
    return lax.dot_general(a, b, (((0,), (0,)), ((), ())), preferred_element_type=F32)


def _ada_kernel(c_ref, w_ref, b_ref, win_ref, o_ref, wmain_ref, wkr_ref, *, n_main):
    c = c_ref[...]
    ca = (c * jax.nn.sigmoid(c)).astype(BF16)
    o_ref[...] = jnp.dot(ca, w_ref[...].astype(BF16), preferred_element_type=F32) + b_ref[...]
    win = win_ref[...]
    wmain_ref[...] = win[:, :n_main].astype(BF16)
    n_kr = win.shape[1] - n_main
    wkr_ref[:, 0:n_kr] = win[:, n_main:].astype(BF16)
    wkr_ref[:, n_kr:] = jnp.zeros((win.shape[0], LANES - n_kr), BF16)


def _ada_mod(c, w_ada, b_ada, w_in, n_main):
    d = c.shape[1]
    n = w_ada.shape[1]
    tn = 1536
    steps = n // tn
    rows = w_in.shape[0] // steps
    assert steps * tn == n and rows * steps == w_in.shape[0] and rows % BF16_ROWS == 0
    c8 = jnp.broadcast_to(c, (8, d))
    out, w_main, w_kr = pl.pallas_call(
        functools.partial(_ada_kernel, n_main=n_main),
        grid=(steps,),
        in_specs=[pl.BlockSpec((8, d), lambda j: (0, 0)),
                  pl.BlockSpec((d, tn), lambda j: (0, j)),
                  pl.BlockSpec((1, tn), lambda j: (0, j)),
                  pl.BlockSpec((rows, w_in.shape[1]), lambda j: (j, 0))],
        out_specs=[pl.BlockSpec((8, tn), lambda j: (0, j)),
                   pl.BlockSpec((rows, n_main), lambda j: (j, 0)),
                   pl.BlockSpec((rows, LANES), lambda j: (j, 0))],
        out_shape=[jax.ShapeDtypeStruct((8, n), F32),
                   jax.ShapeDtypeStruct((w_in.shape[0], n_main), BF16),
                   jax.ShapeDtypeStruct((w_in.shape[0], LANES), BF16)],
        compiler_params=_cparams(1),
        name="ada_mod",
    )(c8, w_ada, b_ada.reshape(1, n), w_in)
    return out[0:1], w_main, w_kr


def _rope_table_kernel(pos_ref, invf_ref, cos_ref, sin_ref):
    ang = pos_ref[...] * invf_ref[...]
    cos_ref[...] = jnp.cos(ang)
    sin_ref[...] = jnp.sin(ang)


def _rope_tables(positions):
    s = positions.shape[0]
    half = QK_ROPE // 2
    rep = LANES // half
    inv_freq = ROPE_THETA ** (-jnp.arange(0, QK_ROPE, 2, dtype=F32) / QK_ROPE)
    pos_rep = jnp.repeat(positions.astype(F32), half).reshape(s // rep, LANES)
    invf = jnp.tile(inv_freq, rep).reshape(1, LANES)
    cos_d, sin_d = pl.pallas_call(
        _rope_table_kernel,
        out_shape=(jax.ShapeDtypeStruct((s // rep, LANES), F32),) * 2,
        name="rope_table",
    )(pos_rep, invf)
    cos = cos_d.reshape(s, half)
    sin = sin_d.reshape(s, half)
    zeros = jnp.zeros((s, LANES - QK_ROPE), F32)
    return (jnp.concatenate([cos, cos, zeros], axis=1),
            jnp.concatenate([-sin, sin, zeros], axis=1))


def _inproj_kernel(x_ref, g_ref, sc_ref, sh_ref, w_ref, wkr_ref, conv_ref, ql_ref, kvl_ref, kr_ref):
    x = x_ref[...]
    ms = jnp.mean(x * x, axis=-1, keepdims=True)
    h = x * lax.rsqrt(ms + EPS) * g_ref[...]
    h = (h * (1.0 + sc_ref[...]) + sh_ref[...]).astype(BF16)
    a = jnp.dot(h, w_ref[...], preferred_element_type=F32)
    o1 = 2 * CONV_CH
    o2 = o1 + Q_LORA
    conv_ref[...] = a[:, :o1]
    ql_ref[...] = a[:, o1:o2]
    kvl_ref[...] = a[:, o2:]
    kr_ref[...] = jnp.dot(h, wkr_ref[...], preferred_element_type=F32)


def _in_proj(x, g, scale, shift, w_main, w_kr, tm):
    s, d = x.shape
    n = w_main.shape[1]
    row = lambda i: (i, 0)
    fixed = lambda i: (0, 0)
    return pl.pallas_call(
        _inproj_kernel,
        grid=(s // tm,),
        in_specs=[pl.BlockSpec((tm, d), row),
                  pl.BlockSpec((1, d), fixed), pl.BlockSpec((1, d), fixed), pl.BlockSpec((1, d), fixed),
                  pl.BlockSpec((d, n), fixed, pipeline_mode=pl.Buffered(1)),
                  pl.BlockSpec((d, LANES), fixed)],
        out_specs=[pl.BlockSpec((tm, 2 * CONV_CH), row), pl.BlockSpec((tm, Q_LORA), row),
                   pl.BlockSpec((tm, KV_LORA), row), pl.BlockSpec((tm, LANES), row)],
        out_shape=[jax.ShapeDtypeStruct((s, 2 * CONV_CH), F32), jax.ShapeDtypeStruct((s, Q_LORA), F32),
                   jax.ShapeDtypeStruct((s, KV_LORA), F32), jax.ShapeDtypeStruct((s, LANES), F32)],
        compiler_params=_cparams(1),
        name="in_proj",
    )(x, g, scale, shift, w_main, w_kr)


def _conv_kernel(cur_ref, halo_ref, w_ref, b_ref, g_ref, beta_ref, wa_ref, wb_ref, o_ref, wa_out, wb_out,
                 hbuf, *, tm, rc):
    i = pl.program_id(0)
    wa_out[...] = wa_ref[...].astype(BF16)
    wb_out[...] = wb_ref[...].astype(BF16)
    cur = cur_ref[...]
    hbuf[pl.ds(CONV_HALO, tm), :] = cur[:, :CONV_CH] * jax.nn.sigmoid(cur[:, CONV_CH:])
    hal = halo_ref[...]
    hg = hal[:, :CONV_CH] * jax.nn.sigmoid(hal[:, CONV_CH:])
    hbuf[pl.ds(0, CONV_HALO), :] = jnp.where(i > 0, hg, 0.0)
    first = CONV_HALO - (CONV_WIDTH - 1)
    span = rc + CONV_HALO

    def row_chunk(r, _):
        r0 = pl.multiple_of(r * rc, rc)
        accs = []
        for c in range(CONV_CH // LANES):
            cols = pl.ds(c * LANES, LANES)
            x = hbuf[pl.ds(r0, span), cols]
            acc = jnp.zeros((rc, LANES), F32)
            for res in range(SUBLANES):
                taps = [k for k in range(CONV_WIDTH) if (first + k) % SUBLANES == res]
                xs = x if res == 0 else pltpu.roll(x, span - res, 0)
                for k in taps:
                    a = (first + k) // SUBLANES
                    acc = acc + w_ref[pl.ds(k, 1), cols] * xs[SUBLANES * a:SUBLANES * a + rc]
            accs.append(acc)
        acc = jnp.concatenate(accs, axis=1) + b_ref[...]
        mu = jnp.mean(acc, axis=-1, keepdims=True)
        cen = acc - mu
        var = jnp.mean(cen * cen, axis=-1, keepdims=True)
        y = cen * lax.rsqrt(var + EPS) * g_ref[...] + beta_ref[...]
        o_ref[pl.ds(r0, rc), :] = (y * jax.nn.sigmoid(y)).astype(BF16)
        return 0

    lax.fori_loop(0, tm // rc, row_chunk, 0)


def _conv_module(a_conv, dw_w, dw_b, ln_g, ln_b, wa, wb, tm):
    s = a_conv.shape[0]
    steps = s // tm
    hb = tm // CONV_HALO
    fixed = lambda i: (0, 0)
    slabs = []
    for w in (wa, wb):
        rows = w.shape[0] // steps
        assert rows * steps == w.shape[0] and rows % BF16_ROWS == 0
        slabs.append(pl.BlockSpec((rows, w.shape[1]), lambda i: (i, 0)))
    return pl.pallas_call(
        functools.partial(_conv_kernel, tm=tm, rc=32),
        grid=(steps,),
        in_specs=[pl.BlockSpec((tm, 2 * CONV_CH), lambda i: (i, 0)),
                  pl.BlockSpec((CONV_HALO, 2 * CONV_CH), lambda i: (jnp.maximum(i * hb - 1, 0), 0)),
                  pl.BlockSpec((CONV_WIDTH, CONV_CH), fixed),
                  pl.BlockSpec((1, CONV_CH), fixed), pl.BlockSpec((1, CONV_CH), fixed),
                  pl.BlockSpec((1, CONV_CH), fixed)] + slabs,
        out_specs=[pl.BlockSpec((tm, CONV_CH), lambda i: (i, 0))] + slabs,
        out_shape=[jax.ShapeDtypeStruct((s, CONV_CH), BF16),
                   jax.ShapeDtypeStruct(wa.shape, BF16), jax.ShapeDtypeStruct(wb.shape, BF16)],
        scratch_shapes=[pltpu.VMEM((tm + CONV_HALO, CONV_CH), F32)],
        compiler_params=_cparams(1),
        name="conv_module",
    )(a_conv, a_conv, dw_w, dw_b, ln_g, ln_b, wa, wb)


def _mla_proj_kernel(ql_ref, kvl_ref, kr_ref, cos_ref, sin_ref, gqa_ref, gkva_ref, wuq_ref, wukv_ref,
                     gq_ref, gk_ref, q_out, k_out, vt_out):
    def rms(v, g):
        return v * lax.rsqrt(jnp.mean(v * v, axis=-1, keepdims=True) + EPS) * g

    q = jnp.dot(rms(ql_ref[...], gqa_ref[...]).astype(BF16), wuq_ref[...], preferred_element_type=F32)
    kv = jnp.dot(rms(kvl_ref[...], gkva_ref[...]).astype(BF16), wukv_ref[...], preferred_element_type=F32)
    kr = kr_ref[...]
    cos = cos_ref[...]
    sin = sin_ref[...]
    lane = lax.broadcasted_iota(jnp.int32, kr.shape, 1)
    half = QK_ROPE // 2

    def rope(v):
        rot = jnp.where(lane < half, pltpu.roll(v, LANES - half, 1), pltpu.roll(v, half, 1))
        return v * cos + rot * sin

    gq = gq_ref[...]
    gk = gk_ref[...]
    scale = QK_HEAD ** -0.5 * LOG2_E
    ss_kr = jnp.sum(kr * kr, axis=-1, keepdims=True)
    for h in range(MLA_HEADS):
        qh = q[:, h * HEAD_PAD:(h + 1) * HEAD_PAD]
        rq = lax.rsqrt(jnp.sum(qh * qh, axis=-1, keepdims=True) / QK_HEAD + EPS)
        qh = qh * rq * gq
        q_out[h, :, 0:QK_NOPE] = (qh[:, :QK_NOPE] * scale).astype(BF16)
        q_out[h, :, QK_NOPE:HEAD_PAD] = (rope(qh[:, QK_NOPE:]) * scale).astype(BF16)
        kn = kv[:, h * HEAD_PAD:h * HEAD_PAD + QK_NOPE]
        vh = kv[:, h * HEAD_PAD + QK_NOPE:(h + 1) * HEAD_PAD]
        rk = lax.rsqrt((jnp.sum(kn * kn, axis=-1, keepdims=True) + ss_kr) / QK_HEAD + EPS)
        k_out[h, :, 0:QK_NOPE] = (kn * rk * gk[:, :QK_NOPE]).astype(BF16)
        k_out[h, :, QK_NOPE:HEAD_PAD] = rope(kr * rk * gk[:, QK_NOPE:]).astype(BF16)
        vt_out[h] = vh.T.astype(BF16)


def _mla_proj(q_lat, kv_lat, k_rope, cos_t, sin_t, gqa, gkva, wuq_pad, wukv, gq_pad, gk_pad, tm):
    s = q_lat.shape[0]
    row = lambda i: (i, 0)
    fixed = lambda i: (0, 0)
    hq = MLA_HEADS * HEAD_PAD
    return pl.pallas_call(
        _mla_proj_kernel,
        grid=(s // tm,),
        in_specs=[pl.BlockSpec((tm, Q_LORA), row), pl.BlockSpec((tm, KV_LORA), row),
                  pl.BlockSpec((tm, LANES), row), pl.BlockSpec((tm, LANES), row), pl.BlockSpec((tm, LANES), row),
                  pl.BlockSpec((1, Q_LORA), fixed), pl.BlockSpec((1, KV_LORA), fixed),
                  pl.BlockSpec((Q_LORA, hq), fixed), pl.BlockSpec((KV_LORA, hq), fixed),
                  pl.BlockSpec((1, HEAD_PAD), fixed), pl.BlockSpec((1, HEAD_PAD), fixed)],
        out_specs=[pl.BlockSpec((MLA_HEADS, tm, HEAD_PAD), lambda i: (0, i, 0)),
                   pl.BlockSpec((MLA_HEADS, tm, HEAD_PAD), lambda i: (0, i, 0)),
                   pl.BlockSpec((MLA_HEADS, V_HEAD, tm), lambda i: (0, 0, i))],
        out_shape=[jax.ShapeDtypeStruct((MLA_HEADS, s, HEAD_PAD), BF16),
                   jax.ShapeDtypeStruct((MLA_HEADS, s, HEAD_PAD), BF16),
                   jax.ShapeDtypeStruct((MLA_HEADS, V_HEAD, s), BF16)],
        compiler_params=_cparams(1),
        name="mla_proj",
    )(q_lat, kv_lat, k_rope, cos_t, sin_t, gqa, gkva, wuq_pad, wukv, gq_pad, gk_pad)


def _attn_kernel(q_ref, k_ref, vt_ref, u_ref, v_ref, o_ref, ub_ref, vb_ref, s_scr, mt_scr, m_scr, l_scr,
                 acc_scr, *, tq, heads):
    qi = pl.program_id(1)
    ub_ref[...] = u_ref[...].astype(BF16)
    vb_ref[...] = v_ref[...].astype(BF16)

    def produce(kt, slot, masked):
        k0 = pl.multiple_of(kt * tq, tq)
        for hh in range(heads):
            s = _nt_dot(k_ref[hh, pl.ds(k0, tq), :], q_ref[hh])
            if masked:
                kc = (k0 + lax.broadcasted_iota(jnp.int32, s.shape, 0)) // CHUNK
                qc = (qi * tq + lax.broadcasted_iota(jnp.int32, s.shape, 1)) // CHUNK
                s = jnp.where(kc <= qc, s, NEG_INF)
            s_scr[hh, slot] = s
            mt_scr[hh, slot] = jnp.max(s, axis=0, keepdims=True)

    def consume(kt, slot):
        k0 = pl.multiple_of(kt * tq, tq)
        for hh in range(heads):
            m = m_scr[hh]
            m_new = jnp.maximum(m, mt_scr[hh, slot])
            alpha = jnp.exp2(m - m_new)
            p = jnp.exp2(s_scr[hh, slot] - m_new)
            l_scr[hh] = alpha * l_scr[hh] + jnp.sum(p, axis=0, keepdims=True)
            pv = jnp.dot(vt_ref[hh, :, pl.ds(k0, tq)], p.astype(BF16), preferred_element_type=F32)
            acc_scr[hh] = alpha * acc_scr[hh] + pv
            m_scr[hh] = m_new

    m_scr[...] = jnp.full(m_scr.shape, NEG_INF, F32)
    l_scr[...] = jnp.zeros_like(l_scr)
    acc_scr[...] = jnp.zeros_like(acc_scr)

    @pl.when(qi == 0)
    def _():
        produce(0, 0, True)

    @pl.when(qi > 0)
    def _():
        produce(0, 0, False)

    def body(kt, _):
        slot = lax.rem(kt, 2)
        consume(kt, slot)
        produce(kt + 1, 1 - slot, False)
        return 0

    lax.fori_loop(0, jnp.maximum(qi - 1, 0), body, 0)

    @pl.when(qi > 0)
    def _():
        slot = lax.rem(qi - 1, 2)
        consume(qi - 1, slot)
        produce(qi, 1 - slot, True)

    consume(qi, lax.rem(qi, 2))
    for hh in range(heads):
        o_ref[:, hh * V_HEAD:(hh + 1) * V_HEAD] = (acc_scr[hh] / l_scr[hh]).T.astype(BF16)


def _attention(q, k, vt, peer_u, peer_v, tq, heads=2):
    h, s, _ = q.shape
    n_q = s // tq
    n_exp, d = peer_u.shape
    slab = n_exp // ((h // heads) * n_q)
    assert slab * (h // heads) * n_q == n_exp and slab % BF16_ROWS == 0
    table = pl.BlockSpec((slab, d), lambda g, i: (g * n_q + i, 0))
    return pl.pallas_call(
        functools.partial(_attn_kernel, tq=tq, heads=heads),
        grid=(h // heads, n_q),
        in_specs=[pl.BlockSpec((heads, tq, HEAD_PAD), lambda g, i: (g, i, 0)),
                  pl.BlockSpec((heads, s, HEAD_PAD), lambda g, i: (g, 0, 0)),
                  pl.BlockSpec((heads, V_HEAD, s), lambda g, i: (g, 0, 0)),
                  table, table],
        out_specs=[pl.BlockSpec((tq, heads * V_HEAD), lambda g, i: (i, g)), table, table],
        out_shape=[jax.ShapeDtypeStruct((s, h * V_HEAD), BF16),
                   jax.ShapeDtypeStruct((n_exp, d), BF16), jax.ShapeDtypeStruct((n_exp, d), BF16)],
        scratch_shapes=[pltpu.VMEM((heads, 2, tq, tq), F32), pltpu.VMEM((heads, 2, 1, tq), F32),
                        pltpu.VMEM((heads, 1, tq), F32), pltpu.VMEM((heads, 1, tq), F32),
                        pltpu.VMEM((heads, V_HEAD, tq), F32)],
        compiler_params=_cparams(2),
        name="attention",
    )(q, k, vt, peer_u, peer_v)


def _outproj_kernel(x_ref, conv_ref, attn_ref, w_ref, gate_ref, g_ref, sc_ref, sh_ref, x1_ref, h2_ref,
                    h2t_ref):
    mix = jnp.dot(conv_ref[...], w_ref[0:CONV_CH, :], preferred_element_type=F32)
    mix = mix + jnp.dot(attn_ref[...], w_ref[CONV_CH:, :], preferred_element_type=F32)
    x1 = x_ref[...] + gate_ref[...] * mix
    x1_ref[...] = x1
    ms = jnp.mean(x1 * x1, axis=-1, keepdims=True)
    h2 = x1 * lax.rsqrt(ms + EPS) * g_ref[...]
    h2 = h2 * (1.0 + sc_ref[...]) + sh_ref[...]
    h2_ref[...] = h2.astype(BF16)
    h2t_ref[...] = h2.T.astype(BF16)


def _out_proj(x, conv_out, attn_out, w_out, gate1, g2, scale2, shift2, tm):
    s, d = x.shape
    kdim = w_out.shape[0]
    row = lambda i: (i, 0)
    fixed = lambda i: (0, 0)
    return pl.pallas_call(
        _outproj_kernel,
        grid=(s // tm,),
        in_specs=[pl.BlockSpec((tm, d), row), pl.BlockSpec((tm, CONV_CH), row),
                  pl.BlockSpec((tm, kdim - CONV_CH), row),
                  pl.BlockSpec((kdim, d), fixed, pipeline_mode=pl.Buffered(1)),
                  pl.BlockSpec((1, d), fixed), pl.BlockSpec((1, d), fixed),
                  pl.BlockSpec((1, d), fixed), pl.BlockSpec((1, d), fixed)],
        out_specs=[pl.BlockSpec((tm, d), row), pl.BlockSpec((tm, d), row),
                   pl.BlockSpec((d, tm), lambda i: (0, i))],
        out_shape=[jax.ShapeDtypeStruct((s, d), F32), jax.ShapeDtypeStruct((s, d), BF16),
                   jax.ShapeDtypeStruct((d, s), BF16)],
        compiler_params=_cparams(1),
        name="out_proj",
    )(x, conv_out, attn_out, w_out, gate1, g2, scale2, shift2)


def _peer_scores_into(h2_ref, wq_ref, k1_ref, k2_ref, s1_ref, s2_ref):
    q = jnp.dot(h2_ref[...], wq_ref[...], preferred_element_type=F32).astype(BF16)
    for h in range(PEER_HEADS):
        base = h * 2 * PEER_HALF
        s1_ref[h] = _nt_dot(k1_ref[h], q[:, base:base + PEER_HALF])
        s2_ref[h] = _nt_dot(k2_ref[h], q[:, base + PEER_HALF:base + 2 * PEER_HALF])


_CAND_ROWS = 80
SELECT_LANES = 512


def _candidate_table():
    tbl = np.full((_CAND_ROWS,), -1, np.int32)
    for i in range(PEER_TOPK):
        tbl[i] = i * PEER_TOPK
    for j in range(1, 8):
        for i in range(8):
            if (i + 1) * (j + 1) <= PEER_TOPK:
                tbl[16 + 8 * (j - 1) + i] = i * PEER_TOPK + j
    for j in range(8, PEER_TOPK):
        tbl[72 + j - 8] = j
    return np.broadcast_to(tbl[:, None], (_CAND_ROWS, SELECT_LANES)).copy()


def _top16_rows(s, exact_ties):
    row = lax.broadcasted_iota(jnp.int32, s.shape, 0) if exact_ties else None
    rank = jnp.full(s.shape, float(PEER_TOPK), F32)
    vals = []
    for r in range(PEER_TOPK):
        m = jnp.max(s, axis=0, keepdims=True)
        if exact_ties:
            first = jnp.min(jnp.where(s == m, row, N_KEYS), axis=0, keepdims=True)
            sel = row == first
        else:
            sel = s == m
        rank = jnp.where(sel, float(r), rank)
        s = jnp.where(sel, NEG_INF, s)
        vals.append(m)
    return jnp.concatenate(vals, axis=0), rank


def _count_off(marked):
    cnt = jnp.sum(jnp.where(marked, 1.0, 0.0), axis=0, keepdims=True)
    return jnp.where(cnt != float(PEER_TOPK), 1.0, 0.0)


def _select_chunk(s1, s2, flat, exact_ties):
    valid = flat >= 0
    row8 = lax.broadcasted_iota(jnp.int32, (8, SELECT_LANES), 0)
    v1, rank1 = _top16_rows(s1, exact_ties)
    v2, rank2 = _top16_rows(s2, exact_ties)
    blocks = [v1 + v2[0:1]]
    for j in range(1, 8):
        blocks.append(v1[0:8] + v2[j:j + 1])
    blocks.append(v1[0:1] + v2[8:16])
    cand0 = jnp.where(valid, jnp.concatenate(blocks, axis=0), NEG_INF)
    cand = cand0
    for _ in range(PEER_TOPK):
        m = jnp.max(cand, axis=0, keepdims=True)
        if exact_ties:
            first = jnp.min(jnp.where(cand == m, flat, PEER_TOPK * PEER_TOPK), axis=0, keepdims=True)
            sel = flat == first
        else:
            sel = cand == m
        cand = jnp.where(sel, NEG_INF, cand)
    chosen = jnp.where(cand == NEG_INF, jnp.where(valid, 1.0, 0.0), 0.0)
    n_lo = chosen[0:8]
    for j in range(1, 8):
        n_lo = n_lo + chosen[16 + 8 * (j - 1):16 + 8 * j]
    tail = jnp.sum(chosen[72:80], axis=0, keepdims=True)
    n_lo = n_lo + jnp.where(row8 == 0, tail, 0.0)
    n = jnp.concatenate([n_lo, chosen[8:16]], axis=0)
    top = v1[0:1] + v2[0:1]
    z = jnp.sum(jnp.where(chosen > 0.0, jnp.exp(cand0 - top), 0.0), axis=0, keepdims=True)
    n1 = jnp.zeros(s1.shape, F32)
    for r in range(PEER_TOPK):
        n1 = jnp.where(rank1 == float(r), n[r:r + 1], n1)
    off = (_count_off(rank1 < float(PEER_TOPK)) + _count_off(rank2 < float(PEER_TOPK))
           + _count_off(chosen > 0.0))
    outs = (n1 * RANK_SCALE, jnp.exp(s1 - v1[0:1]), (rank2 * RANK_SCALE).astype(BF16),
            (jnp.exp(s2 - v2[0:1]) / z).astype(BF16))
    return outs, off


def _peer_route_kernel(h2_ref, wq_ref, k1_ref, k2_ref, tbl_ref, n1_ref, a1_ref, r2_ref, a2_ref,
                       s1_scr, s2_scr, *, chunks):
    _peer_scores_into(h2_ref, wq_ref, k1_ref, k2_ref, s1_scr, s2_scr)
    flat = tbl_ref[...]
    out_refs = (n1_ref, a1_ref, r2_ref, a2_ref)

    def chunk(it, _):
        h = it // chunks
        c = it - h * chunks
        sl = pl.ds(pl.multiple_of(c * SELECT_LANES, SELECT_LANES), SELECT_LANES)
        outs, off = _select_chunk(s1_scr[h, :, sl], s2_scr[h, :, sl], flat, exact_ties=False)
        for ref, val in zip(out_refs, outs):
            ref[h, :, sl] = val

        @pl.when(jnp.sum(off) > 0.0)
        def _():
            outs_exact, _ = _select_chunk(s1_scr[h, :, sl], s2_scr[h, :, sl], flat, exact_ties=True)
            for ref, val in zip(out_refs, outs_exact):
                ref[h, :, sl] = val

        return 0

    lax.fori_loop(0, PEER_HEADS * chunks, chunk, 0)


def _peer_route(h2, wq, k1, k2, tm):
    t, d = h2.shape
    spec = pl.BlockSpec((PEER_HEADS, N_KEYS, tm), lambda i: (0, 0, i))
    keys = pl.BlockSpec((PEER_HEADS, N_KEYS, PEER_HALF), lambda i: (0, 0, 0))
    wide = jax.ShapeDtypeStruct((PEER_HEADS, N_KEYS, t), F32)
    narrow = jax.ShapeDtypeStruct((PEER_HEADS, N_KEYS, t), BF16)
    tbl = jnp.asarray(_candidate_table())
    return pl.pallas_call(
        functools.partial(_peer_route_kernel, chunks=tm // SELECT_LANES),
        grid=(t // tm,),
        in_specs=[pl.BlockSpec((tm, d), lambda i: (i, 0)),
                  pl.BlockSpec(wq.shape, lambda i: (0, 0), pipeline_mode=pl.Buffered(1)),
                  keys, keys, pl.BlockSpec((_CAND_ROWS, SELECT_LANES), lambda i: (0, 0))],
        out_specs=[spec] * 4,
        out_shape=[wide, wide, narrow, narrow],
        scratch_shapes=[pltpu.VMEM((PEER_HEADS, N_KEYS, tm), F32)] * 2,
        compiler_params=_cparams(1),
        name="peer_route",
    )(h2, wq, k1, k2, tbl)


def _gelu_tanh(x):
    return 0.5 * x * (1.0 + jnp.tanh(0.7978845608028654 * (x + 0.044715 * (x * x * x))))


def _zero_of(x):
    bits = lax.bitcast_convert_type(x, jnp.uint32)
    bits = lax.shift_right_logical(lax.shift_right_logical(bits, jnp.uint32(16)), jnp.uint32(16))
    return lax.bitcast_convert_type(bits, F32)


BF16_ROWS = 16


def _rows_bf16(row, n):
    tile = jnp.broadcast_to(row, (BF16_ROWS, row.shape[1])).astype(BF16)
    return jnp.tile(tile, (n // BF16_ROWS, 1))


def _peer_dense_kernel(h2t_ref, u_ref, v_ref, n1_ref, a1_ref, r2_ref, a2_ref, x1_ref, gate_ref, o_ref,
                       *a_scr, sizes):
    j = pl.program_id(1)

    @pl.when(j == 0)
    def _():
        o_ref[...] = jnp.zeros_like(o_ref)

    h2t = h2t_ref[...]
    zero = jnp.zeros((), BF16)
    n_chunks = len(sizes)
    starts = [sum(sizes[:q]) for q in range(n_chunks)]
    slot_w = lax.rem(j, 2)
    slot_r = lax.rem(j + 2, 2)

    def scores(q):
        a_scr[q][slot_w] = jnp.dot(u_ref[pl.ds(starts[q], sizes[q]), :], h2t, preferred_element_type=F32)

    scores(0)
    for q in range(n_chunks):
        chunk = sizes[q]
        per = chunk // N_KEYS
        if q + 1 < n_chunks:
            scores(q + 1)
        blocks = []
        for e in range(per):
            e1 = starts[q] // N_KEYS + e
            pieces = []
            for c in range(h2t.shape[1] // LANES):
                cols = pl.ds(c * LANES, LANES)
                terms = []
                for h in range(PEER_HEADS):
                    n1 = _rows_bf16(n1_ref[h, pl.ds(e1, 1), cols], N_KEYS)
                    a1 = _rows_bf16(a1_ref[h, pl.ds(e1, 1), cols], N_KEYS)
                    keys = pl.ds(h * N_KEYS, N_KEYS)
                    chosen = jnp.maximum(n1 - r2_ref[keys, cols], zero)
                    terms.append(jnp.minimum(a1 * a2_ref[keys, cols], chosen))
                while len(terms) > 1:
                    terms = [terms[k] + terms[k + 1] for k in range(0, len(terms), 2)]
                a_t = a_scr[q][slot_r, pl.ds(e * N_KEYS, N_KEYS), cols]
                pieces.append(terms[0] * _gelu_tanh(a_t.astype(BF16)))
            blocks.append(jnp.concatenate(pieces, axis=1))
        act = jnp.concatenate(blocks, axis=0)
        if q + 1 < n_chunks:
            act = act + _zero_of(a_scr[q + 1][slot_r, 0:1, :]).astype(BF16)
        o_ref[...] += _tn_dot(act, v_ref[pl.ds(starts[q], chunk), :])

    @pl.when(j == pl.num_programs(1) - 1)
    def _():
        o_ref[...] = x1_ref[...] + gate_ref[...] * o_ref[...]


def _peer_dense(h2t, u, v, n1, a1, r2, a2, x1, gate2, tm, te):
    d, t = h2t.shape
    n_j = u.shape[0] // te
    e1_per_tile = te // N_KEYS
    small = pl.BlockSpec((PEER_HEADS, e1_per_tile, tm), lambda i, j: (0, j, i))
    big = pl.BlockSpec((PEER_HEADS * N_KEYS, tm), lambda i, j: (0, i))
    r2 = r2.reshape(PEER_HEADS * N_KEYS, t)
    a2 = a2.reshape(PEER_HEADS * N_KEYS, t)
    sizes = (te // 2, te // 2)
    return pl.pallas_call(
        functools.partial(_peer_dense_kernel, sizes=sizes),
        grid=(t // tm, n_j),
        in_specs=[pl.BlockSpec((d, tm), lambda i, j: (0, i)),
                  pl.BlockSpec((te, d), lambda i, j: (j, 0)),
                  pl.BlockSpec((te, d), lambda i, j: (j, 0)),
                  small, small, big, big,
                  pl.BlockSpec((tm, d), lambda i, j: (i, 0)),
                  pl.BlockSpec((1, d), lambda i, j: (0, 0))],
        out_specs=pl.BlockSpec((tm, d), lambda i, j: (i, 0)),
        out_shape=jax.ShapeDtypeStruct((t, d), F32),
        scratch_shapes=[pltpu.VMEM((2, size, tm), F32) for size in sizes],
        compiler_params=_cparams(2),
        name="peer_dense",
    )(h2t, u, v, n1, a1, r2, a2, x1, gate2)


def _pad_cols(w, width):
    return jnp.pad(w, ((0, 0), (0, width - w.shape[1])))


def _head_pad_cols(w, per_head):
    k = w.shape[0]
    w = w.reshape(k, MLA_HEADS, per_head)
    w = jnp.pad(w, ((0, 0), (0, 0), (0, HEAD_PAD - per_head)))
    return w.reshape(k, MLA_HEADS * HEAD_PAD)


def _layer(x, c, positions_tables, p, tiles):
    d = x.shape[1]
    n_main = 2 * CONV_CH + Q_LORA + KV_LORA
    mod, w_main, w_kr = _ada_mod(c, p["w_ada"], p["b_ada"], p["w_in"], n_main)
    shift1, scale1, gate1, shift2, scale2, gate2 = [mod[:, i * d:(i + 1) * d] for i in range(6)]
    cos_t, sin_t = positions_tables

    a_conv, q_lat, kv_lat, k_rope = _in_proj(x, p["norm1_g"], scale1, shift1, w_main, w_kr, tiles["proj"])

    conv_out, w_out, peer_wq = _conv_module(a_conv, p["conv_dw_w"], p["conv_dw_b"], p["conv_ln_g"],
                                            p["conv_ln_b"], p["w_out"], p["peer_wq"], tiles["conv"])

    wuq = _head_pad_cols(p["w_uq"], QK_HEAD).astype(BF16)
    gq = _pad_cols(p["q_norm_g"], HEAD_PAD)
    gk = _pad_cols(p["k_norm_g"], HEAD_PAD)
    q, k, vt = _mla_proj(q_lat, kv_lat, k_rope, cos_t, sin_t, p["q_a_norm_g"], p["kv_a_norm_g"],
                         wuq, p["w_ukv"].astype(BF16), gq, gk, tiles["proj"])
    attn_out, u_bf16, v_bf16 = _attention(q, k, vt, p["peer_u"], p["peer_v"], tiles["attn"])

    x1, h2, h2t = _out_proj(x, conv_out, attn_out, w_out, gate1, p["norm2_g"], scale2, shift2,
                            tiles["proj"])

    n1, a1, r2, a2 = _peer_route(h2, peer_wq, p["peer_k1"].astype(BF16), p["peer_k2"].astype(BF16),
                                 tiles["proj"])
    return _peer_dense(h2t, u_bf16, v_bf16, n1, a1, r2, a2, x1, gate2,
                       tiles["dense_tokens"], tiles["dense_experts"])


def _tiles(s):
    return {"proj": min(512, s), "conv": min(256, s), "attn": min(512, s),
            "dense_tokens": min(512, s), "dense_experts": 1024}


def kernel(x, c, positions, w_ada, b_ada, norm1_g, w_in, conv_dw_w, conv_dw_b, conv_ln_g, conv_ln_b,
           q_a_norm_g, w_uq, kv_a_norm_g, w_ukv, q_norm_g, k_norm_g, w_out, norm2_g, peer_wq, peer_k1,
           peer_k2, peer_u, peer_v):
    b, s, d = x.shape
    assert b == 1, "kernel is written for batch 1"
    depth = w_ada.shape[0]
    row = lambda a, l: a[l].reshape(1, -1)
    tables = _rope_tables(positions[0])
    xs = x[0]
    for l in range(depth):
        params = {
            "w_ada": w_ada[l], "b_ada": b_ada[l],
            "norm1_g": row(norm1_g, l), "w_in": w_in[l], "conv_dw_w": conv_dw_w[l],
            "conv_dw_b": row(conv_dw_b, l), "conv_ln_g": row(conv_ln_g, l), "conv_ln_b": row(conv_ln_b, l),
            "q_a_norm_g": row(q_a_norm_g, l), "w_uq": w_uq[l], "kv_a_norm_g": row(kv_a_norm_g, l),
            "w_ukv": w_ukv[l], "q_norm_g": row(q_norm_g, l), "k_norm_g": row(k_norm_g, l),
            "w_out": w_out[l], "norm2_g": row(norm2_g, l), "peer_wq": peer_wq[l],
            "peer_k1": peer_k1[l], "peer_k2": peer_k2[l], "peer_u": peer_u[l], "peer_v": peer_v[l],
        }
        xs = _layer(xs, c, tables, params, _tiles(s))
    return xs[None]
```

```python
import functools

import numpy as np
import jax
import jax.numpy as jnp
from jax import lax
from jax.experimental import pallas as pl
from jax.experimental.pallas import tpu as pltpu

F32 = jnp.float32
BF16 = jnp.bfloat16

CHUNK = 64
EPS = 1e-6
CONV_CH = 1024
CONV_WIDTH = 31
MLA_HEADS = 8
QK_NOPE = 128
QK_ROPE = 64
QK_HEAD = QK_NOPE + QK_ROPE
V_HEAD = 128
Q_LORA = 768
KV_LORA = 512
ROPE_THETA = 10000.0
PEER_HEADS = 8
N_KEYS = 128
PEER_TOPK = 16
PEER_HALF = 128

LANES = 128
SUBLANES = 8
HEAD_PAD = 256
CONV_HALO = 32
VMEM_LIMIT = 56 * 1024 * 1024

RANK_SCALE = 1024.0
NEG_INF = float("-inf")
LOG2_E = 1.4426950408889634


def _cparams(n_axes, vmem=VMEM_LIMIT, flags=None):
    return pltpu.CompilerParams(dimension_semantics=("arbitrary",) * n_axes, vmem_limit_bytes=vmem,
                                flags=flags)


def _nt_dot(a, b):
    return lax.dot_general(a, b, (((1,), (1,)), ((), ())), preferred_element_type=F32)


def _tn_dot(a, b):
    return lax.dot_general(a, b, (((0,), (0,)), ((), ())), preferred_element_type=F32)


def _ada_kernel(c_ref, w_ref, b_ref, o_ref):
    c = c_ref[...]
    ca = (c * jax.nn.sigmoid(c)).astype(BF16)
    o_ref[...] = jnp.dot(ca, w_ref[...].astype(BF16), preferred_element_type=F32) + b_ref[...]


def _ada_mod(c, w_ada, b_ada):
    d = c.shape[1]
    n = w_ada.shape[1]
    tn = 1024
    c8 = jnp.broadcast_to(c, (8, d))
    out = pl.pallas_call(
        _ada_kernel,
        grid=(n // tn,),
        in_specs=[pl.BlockSpec((8, d), lambda j: (0, 0)),
                  pl.BlockSpec((d, tn), lambda j: (0, j)),
                  pl.BlockSpec((1, tn), lambda j: (0, j))],
        out_specs=pl.BlockSpec((8, tn), lambda j: (0, j)),
        out_shape=jax.ShapeDtypeStruct((8, n), F32),
        compiler_params=_cparams(1),
        name="ada_mod",
    )(c8, w_ada, b_ada.reshape(1, n))
    return out[0:1]


def _rope_table_kernel(pos_ref, invf_ref, cos_ref, sin_ref):
    ang = pos_ref[...] * invf_ref[...]
    cos_ref[...] = jnp.cos(ang)
    sin_ref[...] = jnp.sin(ang)


def _rope_tables(positions):
    s = positions.shape[0]
    half = QK_ROPE // 2
    rep = LANES // half
    inv_freq = ROPE_THETA ** (-jnp.arange(0, QK_ROPE, 2, dtype=F32) / QK_ROPE)
    pos_rep = jnp.repeat(positions.astype(F32), half).reshape(s // rep, LANES)
    invf = jnp.tile(inv_freq, rep).reshape(1, LANES)
    cos_d, sin_d = pl.pallas_call(
        _rope_table_kernel,
        out_shape=(jax.ShapeDtypeStruct((s // rep, LANES), F32),) * 2,
        name="rope_table",
    )(pos_rep, invf)
    cos = cos_d.reshape(s, half)
    sin = sin_d.reshape(s, half)
    zeros = jnp.zeros((s, LANES - QK_ROPE), F32)
    return (jnp.concatenate([cos, cos, zeros], axis=1),
            jnp.concatenate([-sin, sin, zeros], axis=1))


def _inproj_kernel(x_ref, g_ref, sc_ref, sh_ref, w_ref, wkr_ref, conv_ref, ql_ref, kvl_ref, kr_ref):
    x = x_ref[...]
    ms = jnp.mean(x * x, axis=-1, keepdims=True)
    h = x * lax.rsqrt(ms + EPS) * g_ref[...]
    h = (h * (1.0 + sc_ref[...]) + sh_ref[...]).astype(BF16)
    a = jnp.dot(h, w_ref[...], preferred_element_type=F32)
    o1 = 2 * CONV_CH
    o2 = o1 + Q_LORA
    conv_ref[...] = a[:, :o1]
    ql_ref[...] = a[:, o1:o2]
    kvl_ref[...] = a[:, o2:]
    kr_ref[...] = jnp.dot(h, wkr_ref[...], preferred_element_type=F32)


def _in_proj(x, g, scale, shift, w_main, w_kr, tm):
    s, d = x.shape
    n = w_main.shape[1]
    row = lambda i: (i, 0)
    fixed = lambda i: (0, 0)
    return pl.pallas_call(
        _inproj_kernel,
        grid=(s // tm,),
        in_specs=[pl.BlockSpec((tm, d), row),
                  pl.BlockSpec((1, d), fixed), pl.BlockSpec((1, d), fixed), pl.BlockSpec((1, d), fixed),
                  pl.BlockSpec((d, n), fixed, pipeline_mode=pl.Buffered(1)),
                  pl.BlockSpec((d, LANES), fixed)],
        out_specs=[pl.BlockSpec((tm, 2 * CONV_CH), row), pl.BlockSpec((tm, Q_LORA), row),
                   pl.BlockSpec((tm, KV_LORA), row), pl.BlockSpec((tm, LANES), row)],
        out_shape=[jax.ShapeDtypeStruct((s, 2 * CONV_CH), F32), jax.ShapeDtypeStruct((s, Q_LORA), F32),
                   jax.ShapeDtypeStruct((s, KV_LORA), F32), jax.ShapeDtypeStruct((s, LANES), F32)],
        compiler_params=_cparams(1),
        name="in_proj",
    )(x, g, scale, shift, w_main, w_kr)


def _conv_kernel(cur_ref, halo_ref, w_ref, b_ref, g_ref, beta_ref, wa_ref, wb_ref, o_ref, wa_out, wb_out,
                 hbuf, *, tm, rc):
    i = pl.program_id(0)
    wa_out[...] = wa_ref[...].astype(BF16)
    wb_out[...] = wb_ref[...].astype(BF16)
    cur = cur_ref[...]
    hbuf[pl.ds(CONV_HALO, tm), :] = cur[:, :CONV_CH] * jax.nn.sigmoid(cur[:, CONV_CH:])
    hal = halo_ref[...]
    hg = hal[:, :CONV_CH] * jax.nn.sigmoid(hal[:, CONV_CH:])
    hbuf[pl.ds(0, CONV_HALO), :] = jnp.where(i > 0, hg, 0.0)
    first = CONV_HALO - (CONV_WIDTH - 1)
    span = rc + CONV_HALO

    def row_chunk(r, _):
        r0 = pl.multiple_of(r * rc, rc)
        accs = []
        for c in range(CONV_CH // LANES):
            cols = pl.ds(c * LANES, LANES)
            x = hbuf[pl.ds(r0, span), cols]
            acc = jnp.zeros((rc, LANES), F32)
            for res in range(SUBLANES):
                taps = [k for k in range(CONV_WIDTH) if (first + k) % SUBLANES == res]
                xs = x if res == 0 else pltpu.roll(x, span - res, 0)
                for k in taps:
                    a = (first + k) // SUBLANES
                    acc = acc + w_ref[pl.ds(k, 1), cols] * xs[SUBLANES * a:SUBLANES * a + rc]
            accs.append(acc)
        acc = jnp.concatenate(accs, axis=1) + b_ref[...]
        mu = jnp.mean(acc, axis=-1, keepdims=True)
        cen = acc - mu
        var = jnp.mean(cen * cen, axis=-1, keepdims=True)
        y = cen * lax.rsqrt(var + EPS) * g_ref[...] + beta_ref[...]
        o_ref[pl.ds(r0, rc), :] = (y * jax.nn.sigmoid(y)).astype(BF16)
        return 0

    lax.fori_loop(0, tm // rc, row_chunk, 0)


def _conv_module(a_conv, dw_w, dw_b, ln_g, ln_b, wa, wb, tm):
    s = a_conv.shape[0]
    steps = s // tm
    hb = tm // CONV_HALO
    fixed = lambda i: (0, 0)
    slabs = []
    for w in (wa, wb):
        rows = w.shape[0] // steps
        assert rows * steps == w.shape[0] and rows % BF16_ROWS == 0
        slabs.append(pl.BlockSpec((rows, w.shape[1]), lambda i: (i, 0)))
    return pl.pallas_call(
        functools.partial(_conv_kernel, tm=tm, rc=32),
        grid=(steps,),
        in_specs=[pl.BlockSpec((tm, 2 * CONV_CH), lambda i: (i, 0)),
                  pl.BlockSpec((CONV_HALO, 2 * CONV_CH), lambda i: (jnp.maximum(i * hb - 1, 0), 0)),
                  pl.BlockSpec((CONV_WIDTH, CONV_CH), fixed),
                  pl.BlockSpec((1, CONV_CH), fixed), pl.BlockSpec((1, CONV_CH), fixed),
                  pl.BlockSpec((1, CONV_CH), fixed)] + slabs,
        out_specs=[pl.BlockSpec((tm, CONV_CH), lambda i: (i, 0))] + slabs,
        out_shape=[jax.ShapeDtypeStruct((s, CONV_CH), BF16),
                   jax.ShapeDtypeStruct(wa.shape, BF16), jax.ShapeDtypeStruct(wb.shape, BF16)],
        scratch_shapes=[pltpu.VMEM((tm + CONV_HALO, CONV_CH), F32)],
        compiler_params=_cparams(1),
        name="conv_module",
    )(a_conv, a_conv, dw_w, dw_b, ln_g, ln_b, wa, wb)


def _mla_proj_kernel(ql_ref, kvl_ref, kr_ref, cos_ref, sin_ref, gqa_ref, gkva_ref, wuq_ref, wukv_ref,
                     gq_ref, gk_ref, q_out, k_out, vt_out):
    def rms(v, g):
        return v * lax.rsqrt(jnp.mean(v * v, axis=-1, keepdims=True) + EPS) * g

    q = jnp.dot(rms(ql_ref[...], gqa_ref[...]).astype(BF16), wuq_ref[...], preferred_element_type=F32)
    kv = jnp.dot(rms(kvl_ref[...], gkva_ref[...]).astype(BF16), wukv_ref[...], preferred_element_type=F32)
    kr = kr_ref[...]
    cos = cos_ref[...]
    sin = sin_ref[...]
    lane = lax.broadcasted_iota(jnp.int32, kr.shape, 1)
    half = QK_ROPE // 2

    def rope(v):
        rot = jnp.where(lane < half, pltpu.roll(v, LANES - half, 1), pltpu.roll(v, half, 1))
        return v * cos + rot * sin

    gq = gq_ref[...] * (QK_HEAD ** -0.5 * LOG2_E)
    gk = gk_ref[...]
    ss_kr = jnp.sum(kr * kr, axis=-1, keepdims=True)
    for h in range(MLA_HEADS):
        qh = q[:, h * HEAD_PAD:(h + 1) * HEAD_PAD]
        rq = lax.rsqrt(jnp.sum(qh * qh, axis=-1, keepdims=True) / QK_HEAD + EPS)
        qh = qh * rq * gq
        q_out[h, :, 0:QK_NOPE] = qh[:, :QK_NOPE].astype(BF16)
        q_out[h, :, QK_NOPE:HEAD_PAD] = rope(qh[:, QK_NOPE:]).astype(BF16)
        kn = kv[:, h * HEAD_PAD:h * HEAD_PAD + QK_NOPE]
        vh = kv[:, h * HEAD_PAD + QK_NOPE:(h + 1) * HEAD_PAD]
        rk = lax.rsqrt((jnp.sum(kn * kn, axis=-1, keepdims=True) + ss_kr) / QK_HEAD + EPS)
        k_out[h, :, 0:QK_NOPE] = (kn * rk * gk[:, :QK_NOPE]).astype(BF16)
        k_out[h, :, QK_NOPE:HEAD_PAD] = rope(kr * rk * gk[:, QK_NOPE:]).astype(BF16)
        vt_out[h] = vh.T.astype(BF16)


def _mla_proj(q_lat, kv_lat, k_rope, cos_t, sin_t, gqa, gkva, wuq_pad, wukv, gq_pad, gk_pad, tm):
    s = q_lat.shape[0]
    row = lambda i: (i, 0)
    fixed = lambda i: (0, 0)
    hq = MLA_HEADS * HEAD_PAD
    return pl.pallas_call(
        _mla_proj_kernel,
        grid=(s // tm,),
        in_specs=[pl.BlockSpec((tm, Q_LORA), row), pl.BlockSpec((tm, KV_LORA), row),
                  pl.BlockSpec((tm, LANES), row), pl.BlockSpec((tm, LANES), row), pl.BlockSpec((tm, LANES), row),
                  pl.BlockSpec((1, Q_LORA), fixed), pl.BlockSpec((1, KV_LORA), fixed),
                  pl.BlockSpec((Q_LORA, hq), fixed), pl.BlockSpec((KV_LORA, hq), fixed),
                  pl.BlockSpec((1, HEAD_PAD), fixed), pl.BlockSpec((1, HEAD_PAD), fixed)],
        out_specs=[pl.BlockSpec((MLA_HEADS, tm, HEAD_PAD), lambda i: (0, i, 0)),
                   pl.BlockSpec((MLA_HEADS, tm, HEAD_PAD), lambda i: (0, i, 0)),
                   pl.BlockSpec((MLA_HEADS, V_HEAD, tm), lambda i: (0, 0, i))],
        out_shape=[jax.ShapeDtypeStruct((MLA_HEADS, s, HEAD_PAD), BF16),
                   jax.ShapeDtypeStruct((MLA_HEADS, s, HEAD_PAD), BF16),
                   jax.ShapeDtypeStruct((MLA_HEADS, V_HEAD, s), BF16)],
        compiler_params=_cparams(1),
        name="mla_proj",
    )(q_lat, kv_lat, k_rope, cos_t, sin_t, gqa, gkva, wuq_pad, wukv, gq_pad, gk_pad)


def _attn_kernel(q_ref, k_ref, vt_ref, u_ref, v_ref, o_ref, ub_ref, vb_ref, s_scr, mt_scr, m_scr, l_scr,
                 acc_scr, *, tq, heads):
    qi = pl.program_id(1)
    ub_ref[...] = u_ref[...].astype(BF16)
    vb_ref[...] = v_ref[...].astype(BF16)

    def produce(kt, slot, masked):
        k0 = pl.multiple_of(kt * tq, tq)
        for hh in range(heads):
            s = _nt_dot(k_ref[hh, pl.ds(k0, tq), :], q_ref[hh])
            if masked:
                kc = (k0 + lax.broadcasted_iota(jnp.int32, s.shape, 0)) // CHUNK
                qc = (qi * tq + lax.broadcasted_iota(jnp.int32, s.shape, 1)) // CHUNK
                s = jnp.where(kc <= qc, s, NEG_INF)
            s_scr[hh, slot] = s
            mt_scr[hh, slot] = jnp.max(s, axis=0, keepdims=True)

    def consume(kt, slot):
        k0 = pl.multiple_of(kt * tq, tq)
        for hh in range(heads):
            m = m_scr[hh]
            m_new = jnp.maximum(m, mt_scr[hh, slot])
            alpha = jnp.exp2(m - m_new)
            p = jnp.exp2(s_scr[hh, slot] - m_new)
            l_scr[hh] = alpha * l_scr[hh] + jnp.sum(p, axis=0, keepdims=True)
            pv = jnp.dot(vt_ref[hh, :, pl.ds(k0, tq)], p.astype(BF16), preferred_element_type=F32)
            acc_scr[hh] = alpha * acc_scr[hh] + pv
            m_scr[hh] = m_new

    m_scr[...] = jnp.full(m_scr.shape, NEG_INF, F32)
    l_scr[...] = jnp.zeros_like(l_scr)
    acc_scr[...] = jnp.zeros_like(acc_scr)

    @pl.when(qi == 0)
    def _():
        produce(0, 0, True)

    @pl.when(qi > 0)
    def _():
        produce(0, 0, False)

    def body(kt, _):
        slot = lax.rem(kt, 2)
        consume(kt, slot)
        produce(kt + 1, 1 - slot, False)
        return 0

    lax.fori_loop(0, jnp.maximum(qi - 1, 0), body, 0)

    @pl.when(qi > 0)
    def _():
        slot = lax.rem(qi - 1, 2)
        consume(qi - 1, slot)
        produce(qi, 1 - slot, True)

    consume(qi, lax.rem(qi, 2))
    for hh in range(heads):
        o_ref[:, hh * V_HEAD:(hh + 1) * V_HEAD] = (acc_scr[hh] / l_scr[hh]).T.astype(BF16)


def _attention(q, k, vt, peer_u, peer_v, tq, heads=2):
    h, s, _ = q.shape
    n_q = s // tq
    n_exp, d = peer_u.shape
    slab = n_exp // ((h // heads) * n_q)
    assert slab * (h // heads) * n_q == n_exp and slab % BF16_ROWS == 0
    table = pl.BlockSpec((slab, d), lambda g, i: (g * n_q + i, 0))
    return pl.pallas_call(
        functools.partial(_attn_kernel, tq=tq, heads=heads),
        grid=(h // heads, n_q),
        in_specs=[pl.BlockSpec((heads, tq, HEAD_PAD), lambda g, i: (g, i, 0)),
                  pl.BlockSpec((heads, s, HEAD_PAD), lambda g, i: (g, 0, 0)),
                  pl.BlockSpec((heads, V_HEAD, s), lambda g, i: (g, 0, 0)),
                  table, table],
        out_specs=[pl.BlockSpec((tq, heads * V_HEAD), lambda g, i: (i, g)), table, table],
        out_shape=[jax.ShapeDtypeStruct((s, h * V_HEAD), BF16),
                   jax.ShapeDtypeStruct((n_exp, d), BF16), jax.ShapeDtypeStruct((n_exp, d), BF16)],
        scratch_shapes=[pltpu.VMEM((heads, 2, tq, tq), F32), pltpu.VMEM((heads, 2, 1, tq), F32),
                        pltpu.VMEM((heads, 1, tq), F32), pltpu.VMEM((heads, 1, tq), F32),
                        pltpu.VMEM((heads, V_HEAD, tq), F32)],
        compiler_params=_cparams(2),
        name="attention",
    )(q, k, vt, peer_u, peer_v)


def _outproj_kernel(x_ref, conv_ref, attn_ref, w_ref, gate_ref, g_ref, sc_ref, sh_ref, x1_ref, h2_ref,
                    h2t_ref):
    mix = jnp.dot(conv_ref[...], w_ref[0:CONV_CH, :], preferred_element_type=F32)
    mix = mix + jnp.dot(attn_ref[...], w_ref[CONV_CH:, :], preferred_element_type=F32)
    x1 = x_ref[...] + gate_ref[...] * mix
    x1_ref[...] = x1
    ms = jnp.mean(x1 * x1, axis=-1, keepdims=True)
    h2 = x1 * lax.rsqrt(ms + EPS) * g_ref[...]
    h2 = h2 * (1.0 + sc_ref[...]) + sh_ref[...]
    h2_ref[...] = h2.astype(BF16)
    h2t_ref[...] = h2.T.astype(BF16)


def _out_proj(x, conv_out, attn_out, w_out, gate1, g2, scale2, shift2, tm):
    s, d = x.shape
    kdim = w_out.shape[0]
    row = lambda i: (i, 0)
    fixed = lambda i: (0, 0)
    return pl.pallas_call(
        _outproj_kernel,
        grid=(s // tm,),
        in_specs=[pl.BlockSpec((tm, d), row), pl.BlockSpec((tm, CONV_CH), row),
                  pl.BlockSpec((tm, kdim - CONV_CH), row),
                  pl.BlockSpec((kdim, d), fixed, pipeline_mode=pl.Buffered(1)),
                  pl.BlockSpec((1, d), fixed), pl.BlockSpec((1, d), fixed),
                  pl.BlockSpec((1, d), fixed), pl.BlockSpec((1, d), fixed)],
        out_specs=[pl.BlockSpec((tm, d), row), pl.BlockSpec((tm, d), row),
                   pl.BlockSpec((d, tm), lambda i: (0, i))],
        out_shape=[jax.ShapeDtypeStruct((s, d), F32), jax.ShapeDtypeStruct((s, d), BF16),
                   jax.ShapeDtypeStruct((d, s), BF16)],
        compiler_params=_cparams(1),
        name="out_proj",
    )(x, conv_out, attn_out, w_out, gate1, g2, scale2, shift2)


def _peer_scores_into(h2_ref, wq_ref, k1_ref, k2_ref, s1_ref, s2_ref):
    q = jnp.dot(h2_ref[...], wq_ref[...], preferred_element_type=F32).astype(BF16)
    for h in range(PEER_HEADS):
        base = h * 2 * PEER_HALF
        s1_ref[h] = _nt_dot(k1_ref[h], q[:, base:base + PEER_HALF])
        s2_ref[h] = _nt_dot(k2_ref[h], q[:, base + PEER_HALF:base + 2 * PEER_HALF])


_CAND_ROWS = 80
SELECT_LANES = 512


def _candidate_table():
    tbl = np.full((_CAND_ROWS,), -1, np.int32)
    for i in range(PEER_TOPK):
        tbl[i] = i * PEER_TOPK
    for j in range(1, 8):
        for i in range(8):
            if (i + 1) * (j + 1) <= PEER_TOPK:
                tbl[16 + 8 * (j - 1) + i] = i * PEER_TOPK + j
    for j in range(8, PEER_TOPK):
        tbl[72 + j - 8] = j
    return np.broadcast_to(tbl[:, None], (_CAND_ROWS, SELECT_LANES)).copy()


def _top16_rows(s, exact_ties):
    row = lax.broadcasted_iota(jnp.int32, s.shape, 0) if exact_ties else None
    rank = jnp.full(s.shape, float(PEER_TOPK), F32)
    vals = []
    for r in range(PEER_TOPK):
        m = jnp.max(s, axis=0, keepdims=True)
        if exact_ties:
            first = jnp.min(jnp.where(s == m, row, N_KEYS), axis=0, keepdims=True)
            sel = row == first
        else:
            sel = s == m
        rank = jnp.where(sel, float(r), rank)
        s = jnp.where(sel, NEG_INF, s)
        vals.append(m)
    return jnp.concatenate(vals, axis=0), rank


def _count_off(marked):
    cnt = jnp.sum(jnp.where(marked, 1.0, 0.0), axis=0, keepdims=True)
    return jnp.where(cnt != float(PEER_TOPK), 1.0, 0.0)


def _select_chunk(s1, s2, flat, exact_ties):
    valid = flat >= 0
    row8 = lax.broadcasted_iota(jnp.int32, (8, SELECT_LANES), 0)
    v1, rank1 = _top16_rows(s1, exact_ties)
    v2, rank2 = _top16_rows(s2, exact_ties)
    blocks = [v1 + v2[0:1]]
    for j in range(1, 8):
        blocks.append(v1[0:8] + v2[j:j + 1])
    blocks.append(v1[0:1] + v2[8:16])
    cand0 = jnp.where(valid, jnp.concatenate(blocks, axis=0), NEG_INF)
    cand = cand0
    for _ in range(PEER_TOPK):
        m = jnp.max(cand, axis=0, keepdims=True)
        if exact_ties:
            first = jnp.min(jnp.where(cand == m, flat, PEER_TOPK * PEER_TOPK), axis=0, keepdims=True)
            sel = flat == first
        else:
            sel = cand == m
        cand = jnp.where(sel, NEG_INF, cand)
    chosen = jnp.where(cand == NEG_INF, jnp.where(valid, 1.0, 0.0), 0.0)
    n_lo = chosen[0:8]
    for j in range(1, 8):
        n_lo = n_lo + chosen[16 + 8 * (j - 1):16 + 8 * j]
    tail = jnp.sum(chosen[72:80], axis=0, keepdims=True)
    n_lo = n_lo + jnp.where(row8 == 0, tail, 0.0)
    n = jnp.concatenate([n_lo, chosen[8:16]], axis=0)
    top = v1[0:1] + v2[0:1]
    z = jnp.sum(jnp.where(chosen > 0.0, jnp.exp(cand0 - top), 0.0), axis=0, keepdims=True)
    n1 = jnp.zeros(s1.shape, F32)
    for r in range(PEER_TOPK):
        n1 = jnp.where(rank1 == float(r), n[r:r + 1], n1)
    off = (_count_off(rank1 < float(PEER_TOPK)) + _count_off(rank2 < float(PEER_TOPK))
           + _count_off(chosen > 0.0))
    outs = (n1 * RANK_SCALE, jnp.exp(s1 - v1[0:1]), (rank2 * RANK_SCALE).astype(BF16),
            (jnp.exp(s2 - v2[0:1]) / z).astype(BF16))
    return outs, off


def _peer_route_kernel(h2_ref, wq_ref, k1_ref, k2_ref, tbl_ref, n1_ref, a1_ref, r2_ref, a2_ref,
                       s1_scr, s2_scr, *, chunks):
    _peer_scores_into(h2_ref, wq_ref, k1_ref, k2_ref, s1_scr, s2_scr)
    flat = tbl_ref[...]
    out_refs = (n1_ref, a1_ref, r2_ref, a2_ref)

    def chunk(it, _):
        h = it // chunks
        c = it - h * chunks
        sl = pl.ds(pl.multiple_of(c * SELECT_LANES, SELECT_LANES), SELECT_LANES)
        outs, off = _select_chunk(s1_scr[h, :, sl], s2_scr[h, :, sl], flat, exact_ties=False)
        for ref, val in zip(out_refs, outs):
            ref[h, :, sl] = val

        @pl.when(jnp.sum(off) > 0.0)
        def _():
            outs_exact, _ = _select_chunk(s1_scr[h, :, sl], s2_scr[h, :, sl], flat, exact_ties=True)
            for ref, val in zip(out_refs, outs_exact):
                ref[h, :, sl] = val

        return 0

    lax.fori_loop(0, PEER_HEADS * chunks, chunk, 0)


def _peer_route(h2, wq, k1, k2, tm):
    t, d = h2.shape
    spec = pl.BlockSpec((PEER_HEADS, N_KEYS, tm), lambda i: (0, 0, i))
    keys = pl.BlockSpec((PEER_HEADS, N_KEYS, PEER_HALF), lambda i: (0, 0, 0))
    wide = jax.ShapeDtypeStruct((PEER_HEADS, N_KEYS, t), F32)
    narrow = jax.ShapeDtypeStruct((PEER_HEADS, N_KEYS, t), BF16)
    tbl = jnp.asarray(_candidate_table())
    return pl.pallas_call(
        functools.partial(_peer_route_kernel, chunks=tm // SELECT_LANES),
        grid=(t // tm,),
        in_specs=[pl.BlockSpec((tm, d), lambda i: (i, 0)),
                  pl.BlockSpec(wq.shape, lambda i: (0, 0), pipeline_mode=pl.Buffered(1)),
                  keys, keys, pl.BlockSpec((_CAND_ROWS, SELECT_LANES), lambda i: (0, 0))],
        out_specs=[spec] * 4,
        out_shape=[wide, wide, narrow, narrow],
        scratch_shapes=[pltpu.VMEM((PEER_HEADS, N_KEYS, tm), F32)] * 2,
        compiler_params=_cparams(1),
        name="peer_route",
    )(h2, wq, k1, k2, tbl)


def _gelu_tanh(x):
    return 0.5 * x * (1.0 + jnp.tanh(0.7978845608028654 * (x + 0.044715 * (x * x * x))))


def _zero_of(x):
    bits = lax.bitcast_convert_type(x, jnp.uint32)
    bits = lax.shift_right_logical(lax.shift_right_logical(bits, jnp.uint32(16)), jnp.uint32(16))
    return lax.bitcast_convert_type(bits, F32)


BF16_ROWS = 16


def _rows_bf16(row, n):
    tile = jnp.broadcast_to(row, (BF16_ROWS, row.shape[1])).astype(BF16)
    return jnp.tile(tile, (n // BF16_ROWS, 1))


def _peer_dense_kernel(h2t_ref, u_ref, v_ref, n1_ref, a1_ref, r2_ref, a2_ref, x1_ref, gate_ref, o_ref,
                       *a_scr, sizes):
    j = pl.program_id(1)

    @pl.when(j == 0)
    def _():
        o_ref[...] = jnp.zeros_like(o_ref)

    h2t = h2t_ref[...]
    zero = jnp.zeros((), BF16)
    n_chunks = len(sizes)
    starts = [sum(sizes[:q]) for q in range(n_chunks)]
    slot_w = lax.rem(j, 2)
    slot_r = lax.rem(j + 2, 2)

    def scores(q):
        a_scr[q][slot_w] = jnp.dot(u_ref[pl.ds(starts[q], sizes[q]), :], h2t, preferred_element_type=F32)

    scores(0)
    for q in range(n_chunks):
        chunk = sizes[q]
        per = chunk // N_KEYS
        if q + 1 < n_chunks:
            scores(q + 1)
        blocks = []
        for e in range(per):
            e1 = starts[q] // N_KEYS + e
            pieces = []
            for c in range(h2t.shape[1] // LANES):
                cols = pl.ds(c * LANES, LANES)
                terms = []
                for h in range(PEER_HEADS):
                    n1 = _rows_bf16(n1_ref[h, pl.ds(e1, 1), cols], N_KEYS)
                    a1 = _rows_bf16(a1_ref[h, pl.ds(e1, 1), cols], N_KEYS)
                    keys = pl.ds(h * N_KEYS, N_KEYS)
                    chosen = jnp.maximum(n1 - r2_ref[keys, cols], zero)
                    terms.append(jnp.minimum(a1 * a2_ref[keys, cols], chosen))
                while len(terms) > 1:
                    terms = [terms[k] + terms[k + 1] for k in range(0, len(terms), 2)]
                a_t = a_scr[q][slot_r, pl.ds(e * N_KEYS, N_KEYS), cols]
                pieces.append(terms[0] * _gelu_tanh(a_t.astype(BF16)))
            blocks.append(jnp.concatenate(pieces, axis=1))
        act = jnp.concatenate(blocks, axis=0)
        if q + 1 < n_chunks:
            act = act + _zero_of(a_scr[q + 1][slot_r, 0:1, :]).astype(BF16)
        o_ref[...] += _tn_dot(act, v_ref[pl.ds(starts[q], chunk), :])

    @pl.when(j == pl.num_programs(1) - 1)
    def _():
        o_ref[...] = x1_ref[...] + gate_ref[...] * o_ref[...]


def _peer_dense(h2t, u, v, n1, a1, r2, a2, x1, gate2, tm, te):
    d, t = h2t.shape
    n_j = u.shape[0] // te
    e1_per_tile = te // N_KEYS
    small = pl.BlockSpec((PEER_HEADS, e1_per_tile, tm), lambda i, j: (0, j, i))
    big = pl.BlockSpec((PEER_HEADS * N_KEYS, tm), lambda i, j: (0, i))
    r2 = r2.reshape(PEER_HEADS * N_KEYS, t)
    a2 = a2.reshape(PEER_HEADS * N_KEYS, t)
    sizes = (te // 2, te // 2)
    return pl.pallas_call(
        functools.partial(_peer_dense_kernel, sizes=sizes),
        grid=(t // tm, n_j),
        in_specs=[pl.BlockSpec((d, tm), lambda i, j: (0, i)),
                  pl.BlockSpec((te, d), lambda i, j: (j, 0)),
                  pl.BlockSpec((te, d), lambda i, j: (j, 0)),
                  small, small, big, big,
                  pl.BlockSpec((tm, d), lambda i, j: (i, 0)),
                  pl.BlockSpec((1, d), lambda i, j: (0, 0))],
        out_specs=pl.BlockSpec((tm, d), lambda i, j: (i, 0)),
        out_shape=jax.ShapeDtypeStruct((t, d), F32),
        scratch_shapes=[pltpu.VMEM((2, size, tm), F32) for size in sizes],
        compiler_params=_cparams(2),
        name="peer_dense",
    )(h2t, u, v, n1, a1, r2, a2, x1, gate2)


def _pad_cols(w, width):
    return jnp.pad(w, ((0, 0), (0, width - w.shape[1])))


def _head_pad_cols(w, per_head):
    k = w.shape[0]
    w = w.reshape(k, MLA_HEADS, per_head)
    w = jnp.pad(w, ((0, 0), (0, 0), (0, HEAD_PAD - per_head)))
    return w.reshape(k, MLA_HEADS * HEAD_PAD)


def _layer(x, c, positions_tables, p, tiles):
    d = x.shape[1]
    mod = _ada_mod(c, p["w_ada"], p["b_ada"])
    shift1, scale1, gate1, shift2, scale2, gate2 = [mod[:, i * d:(i + 1) * d] for i in range(6)]
    cos_t, sin_t = positions_tables

    n_main = 2 * CONV_CH + Q_LORA + KV_LORA
    w_main = p["w_in"][:, :n_main].astype(BF16)
    w_kr = _pad_cols(p["w_in"][:, n_main:], LANES).astype(BF16)
    a_conv, q_lat, kv_lat, k_rope = _in_proj(x, p["norm1_g"], scale1, shift1, w_main, w_kr, tiles["proj"])

    conv_out, w_out, peer_wq = _conv_module(a_conv, p["conv_dw_w"], p["conv_dw_b"], p["conv_ln_g"],
                                            p["conv_ln_b"], p["w_out"], p["peer_wq"], tiles["conv"])

    wuq = _head_pad_cols(p["w_uq"], QK_HEAD).astype(BF16)
    gq = _pad_cols(p["q_norm_g"], HEAD_PAD)
    gk = _pad_cols(p["k_norm_g"], HEAD_PAD)
    q, k, vt = _mla_proj(q_lat, kv_lat, k_rope, cos_t, sin_t, p["q_a_norm_g"], p["kv_a_norm_g"],
                         wuq, p["w_ukv"].astype(BF16), gq, gk, tiles["proj"])
    attn_out, u_bf16, v_bf16 = _attention(q, k, vt, p["peer_u"], p["peer_v"], tiles["attn"])

    x1, h2, h2t = _out_proj(x, conv_out, attn_out, w_out, gate1, p["norm2_g"], scale2, shift2,
                            tiles["proj"])

    n1, a1, r2, a2 = _peer_route(h2, peer_wq, p["peer_k1"].astype(BF16), p["peer_k2"].astype(BF16),
                                 tiles["proj"])
    return _peer_dense(h2t, u_bf16, v_bf16, n1, a1, r2, a2, x1, gate2,
                       tiles["dense_tokens"], tiles["dense_experts"])


def _tiles(s):
    return {"proj": min(512, s), "conv": min(512, s), "attn": min(512, s),
            "dense_tokens": min(512, s), "dense_experts": 1024}


def kernel(x, c, positions, w_ada, b_ada, norm1_g, w_in, conv_dw_w, conv_dw_b, conv_ln_g, conv_ln_b,
           q_a_norm_g, w_uq, kv_a_norm_g, w_ukv, q_norm_g, k_norm_g, w_out, norm2_g, peer_wq, peer_k1,
           peer_k2, peer_u, peer_v):
    b, s, d = x.shape
    assert b == 1, "kernel is written for batch 1"
    depth = w_ada.shape[0]
    row = lambda a, l: a[l].reshape(1, -1)
    tables = _rope_tables(positions[0])
    xs = x[0]
    for l in range(depth):
        params = {
            "w_ada": w_ada[l], "b_ada": b_ada[l],
            "norm1_g": row(norm1_g, l), "w_in": w_in[l], "conv_dw_w": conv_dw_w[l],
            "conv_dw_b": row(conv_dw_b, l), "conv_ln_g": row(conv_ln_g, l), "conv_ln_b": row(conv_ln_b, l),
            "q_a_norm_g": row(q_a_norm_g, l), "w_uq": w_uq[l], "kv_a_norm_g": row(kv_a_norm_g, l),
            "w_ukv": w_ukv[l], "q_norm_g": row(q_norm_g, l), "k_norm_g": row(k_norm_g, l),
            "w_out": w_out[l], "norm2_g": row(norm2_g, l), "peer_wq": peer_wq[l],
            "peer_k1": peer_k1[l], "peer_k2": peer_k2[l], "peer_u": peer_u[l], "peer_v": peer_v[l],
        }
        xs = _layer(xs, c, tables, params, _tiles(s))
    return xs[None]
```
